```python
import jax, jax.numpy as jnp
from jax import lax
import numpy as np

D_MODEL = 1024
BATCH = 8
SEQ = 2048
DEPTH = 2
DEC_BATCH = 128
DEC_SEQ = 4
PAST_LEN = 16384
PAGE_SIZE = 128

N_MIXERS = 2
N_A_LAYERS = (DEPTH + 1) // 2
N_B_LAYERS = DEPTH // 2
D_RNN = D_MODEL
N_RG_BLOCKS = 16
RG_BLOCK = D_RNN // N_RG_BLOCKS
RG_CONV_W = 4
RG_C = 8.0
D_CONV = D_MODEL
CONF_CONV_W = 31
N_KEYS = 128
N_EXPERTS = N_KEYS * N_KEYS
N_RHEADS = 8
D_KEY = 256
D_HALF = D_KEY // 2
TOPK = 16
EXPERT_BLOCK = 128
EPS = 1e-6

kernel_name = 'hawk_conformer_peer_hybrid_step'


def rmsnorm(x, g):
    xf = x.astype(jnp.float32)
    y = xf * lax.rsqrt(jnp.mean(xf * xf, axis=-1, keepdims=True) + EPS)
    return (y * g.astype(jnp.float32)).astype(x.dtype)


def layernorm(x, g, b):
    xf = x.astype(jnp.float32)
    mu = jnp.mean(xf, axis=-1, keepdims=True)
    xc = xf - mu
    var = jnp.mean(xc * xc, axis=-1, keepdims=True)
    y = xc * lax.rsqrt(var + EPS) * g.astype(jnp.float32) + b.astype(jnp.float32)
    return y.astype(x.dtype)


def causal_depthwise(x_ext, w, b):
    c = x_ext.shape[-1]
    y = lax.conv_general_dilated(
        x_ext, w[:, None, :].astype(x_ext.dtype), window_strides=(1,), padding='VALID',
        dimension_numbers=('NWC', 'WIO', 'NWC'), feature_group_count=c)
    return y + b.astype(x_ext.dtype)


def rg_lru(x, h0, w_a, b_a, w_i, b_i, lam):
    nb, t, _ = x.shape
    xf = x.astype(jnp.float32)
    xb = xf.reshape(nb, t, N_RG_BLOCKS, RG_BLOCK)
    r = jax.nn.sigmoid(jnp.einsum('btni,nij->btnj', xb, w_a.astype(jnp.float32)).reshape(nb, t, D_RNN)
                       + b_a.astype(jnp.float32))
    i = jax.nn.sigmoid(jnp.einsum('btni,nij->btnj', xb, w_i.astype(jnp.float32)).reshape(nb, t, D_RNN)
                       + b_i.astype(jnp.float32))
    log_a = -RG_C * r * jax.nn.softplus(-lam.astype(jnp.float32))
    a = jnp.exp(log_a)
    u = jnp.sqrt(-jnp.expm1(2.0 * log_a)) * (i * xf)
    u = u.at[:, 0].add(a[:, 0] * h0.astype(jnp.float32))

    def combine(c1, c2):
        a1, b1 = c1
        a2, b2 = c2
        return a1 * a2, a2 * b1 + b2

    _, h = lax.associative_scan(combine, (a, u), axis=1)
    return h


def recurrent_mixer(xn, h0, conv_buf, w_in, conv_w, conv_b, w_a, b_a, w_i, b_i, lam, w_out):
    proj = xn @ w_in
    gate = jax.nn.gelu(proj[..., :D_RNN])
    xr = proj[..., D_RNN:]
    x_ext = jnp.concatenate([conv_buf.astype(xr.dtype), xr], axis=1)
    xc = causal_depthwise(x_ext, conv_w, conv_b)
    h = rg_lru(xc, h0, w_a, b_a, w_i, b_i, lam)
    y = (h.astype(xn.dtype) * gate) @ w_out
    return y, h[:, -1].astype(xn.dtype), x_ext[:, -(RG_CONV_W - 1):]


def conformer_mixer(xn, conv_buf, w_pw1, b_pw1, dw_w, dw_b, ln_g, ln_b, w_pw2, b_pw2):
    p = xn @ w_pw1 + b_pw1
    z = p[..., :D_CONV] * jax.nn.sigmoid(p[..., D_CONV:])
    z_ext = jnp.concatenate([conv_buf.astype(z.dtype), z], axis=1)
    c = causal_depthwise(z_ext, dw_w, dw_b)
    c = jax.nn.silu(layernorm(c, ln_g, ln_b))
    y = c @ w_pw2 + b_pw2
    return y, z_ext[:, -(CONF_CONV_W - 1):]


def peer(xn, w_q, sub_keys, u_tab, v_tab):
    nb, t, d = xn.shape
    xt = xn.reshape(nb * t, d)
    n_tok = xt.shape[0]
    q = (xt @ w_q).astype(jnp.float32).reshape(n_tok, N_RHEADS, 2, D_HALF)
    s = jnp.einsum('thpc,hpkc->thpk', q, sub_keys.astype(jnp.float32))
    sv, si = lax.top_k(s, TOPK)
    cand = (sv[:, :, 0, :, None] + sv[:, :, 1, None, :]).reshape(n_tok, N_RHEADS, TOPK * TOPK)
    cand_idx = (si[:, :, 0, :, None] * N_KEYS + si[:, :, 1, None, :]).reshape(n_tok, N_RHEADS, TOPK * TOPK)
    best, pos = lax.top_k(cand, TOPK)
    eidx = jnp.take_along_axis(cand_idx, pos, axis=-1).reshape(n_tok, N_RHEADS * TOPK)
    g = jax.nn.softmax(best, axis=-1).reshape(n_tok, N_RHEADS * TOPK)
    n_pad = (-n_tok) % EXPERT_BLOCK
    n_blk = (n_tok + n_pad) // EXPERT_BLOCK
    xb = jnp.pad(xt, ((0, n_pad), (0, 0))).reshape(n_blk, EXPERT_BLOCK, d)
    ib = jnp.pad(eidx, ((0, n_pad), (0, 0))).reshape(n_blk, EXPERT_BLOCK, N_RHEADS * TOPK)
    gb = jnp.pad(g, ((0, n_pad), (0, 0))).reshape(n_blk, EXPERT_BLOCK, N_RHEADS * TOPK)

    def expert_block(args):
        xblk, iblk, gblk = args
        u = jnp.take(u_tab, iblk, axis=0)
        act = jax.nn.gelu(jnp.einsum('ted,td->te', u, xblk).astype(jnp.float32))
        w = (gblk * act).astype(xblk.dtype)
        v = jnp.take(v_tab, iblk, axis=0)
        return jnp.einsum('te,ted->td', w, v)

    out = lax.map(expert_block, (xb, ib, gb)).reshape(n_blk * EXPERT_BLOCK, d)[:n_tok]
    return out.reshape(nb, t, d)


def trunk(x, rg_h0, rg_conv0, cf_conv0, p):
    new_h, new_rgc, new_cfc = [], [], []
    for i in range(DEPTH):
        xn = rmsnorm(x, p['norm_mix'][i])
        j = i // N_MIXERS
        if i % N_MIXERS == 0:
            y, h_last, buf = recurrent_mixer(
                xn, rg_h0[j], rg_conv0[j], p['rg_w_in'][j], p['rg_conv_w'][j], p['rg_conv_b'][j],
                p['rg_w_a'][j], p['rg_b_a'][j], p['rg_w_i'][j], p['rg_b_i'][j], p['rg_lambda'][j],
                p['rg_w_out'][j])
            new_h.append(h_last)
            new_rgc.append(buf)
        else:
            y, buf = conformer_mixer(
                xn, cf_conv0[j], p['cf_w_pw1'][j], p['cf_b_pw1'][j], p['cf_dw_w'][j], p['cf_dw_b'][j],
                p['cf_ln_g'][j], p['cf_ln_b'][j], p['cf_w_pw2'][j], p['cf_b_pw2'][j])
            new_cfc.append(buf)
        x = x + y
        x = x + peer(rmsnorm(x, p['norm_ffn'][i]), p['peer_w_q'][i], p['peer_sub_keys'][i],
                     p['peer_u'][i], p['peer_v'][i])
    return rmsnorm(x, p['norm_final']), jnp.stack(new_h), jnp.stack(new_rgc), jnp.stack(new_cfc)


def setup_inputs(seed: int = 0) -> dict:
    key = jax.random.key(seed)
    ks = iter(jax.random.split(key, 40))
    f32 = jnp.float32

    def nrm(shape, scale):
        return jax.random.normal(next(ks), shape, f32) * scale

    x_prompt = nrm((BATCH, SEQ, D_MODEL), 1.0)
    x_sample = nrm((DEC_BATCH, DEC_SEQ, D_MODEL), 1.0)
    state_rglru_h = nrm((N_A_LAYERS, DEC_BATCH, D_RNN), 0.5)
    state_rglru_conv = nrm((N_A_LAYERS, DEC_BATCH, RG_CONV_W - 1, D_RNN), 1.0)
    state_conformer_conv = nrm((N_B_LAYERS, DEC_BATCH, CONF_CONV_W - 1, D_CONV), 0.5)
    norm_mix = 1.0 + nrm((DEPTH, D_MODEL), 0.02)
    norm_ffn = 1.0 + nrm((DEPTH, D_MODEL), 0.02)
    norm_final = 1.0 + nrm((D_MODEL,), 0.02)
    rg_w_in = nrm((N_A_LAYERS, D_MODEL, 2 * D_RNN), D_MODEL ** -0.5)
    rg_conv_w = nrm((N_A_LAYERS, RG_CONV_W, D_RNN), RG_CONV_W ** -0.5)
    rg_conv_b = nrm((N_A_LAYERS, D_RNN), 0.01)
    rg_w_a = nrm((N_A_LAYERS, N_RG_BLOCKS, RG_BLOCK, RG_BLOCK), RG_BLOCK ** -0.5)
    rg_b_a = nrm((N_A_LAYERS, D_RNN), 0.01)
    rg_w_i = nrm((N_A_LAYERS, N_RG_BLOCKS, RG_BLOCK, RG_BLOCK), RG_BLOCK ** -0.5)
    rg_b_i = nrm((N_A_LAYERS, D_RNN), 0.01)
    a_target = jax.random.uniform(next(ks), (N_A_LAYERS, D_RNN), f32, 0.9, 0.999)
    p_sig = a_target ** (1.0 / RG_C)
    rg_lambda = jnp.log(p_sig) - jnp.log1p(-p_sig)
    rg_w_out = nrm((N_A_LAYERS, D_RNN, D_MODEL), D_RNN ** -0.5)
    cf_w_pw1 = nrm((N_B_LAYERS, D_MODEL, 2 * D_CONV), D_MODEL ** -0.5)
    cf_b_pw1 = nrm((N_B_LAYERS, 2 * D_CONV), 0.01)
    cf_dw_w = nrm((N_B_LAYERS, CONF_CONV_W, D_CONV), CONF_CONV_W ** -0.5)
    cf_dw_b = nrm((N_B_LAYERS, D_CONV), 0.01)
    cf_ln_g = 1.0 + nrm((N_B_LAYERS, D_CONV), 0.02)
    cf_ln_b = nrm((N_B_LAYERS, D_CONV), 0.01)
    cf_w_pw2 = nrm((N_B_LAYERS, D_CONV, D_MODEL), D_CONV ** -0.5)
    cf_b_pw2 = nrm((N_B_LAYERS, D_MODEL), 0.01)
    peer_w_q = nrm((DEPTH, D_MODEL, N_RHEADS * D_KEY), D_MODEL ** -0.5)
    peer_sub_keys = nrm((DEPTH, N_RHEADS, 2, N_KEYS, D_HALF), D_HALF ** -0.5)
    peer_u = nrm((DEPTH, N_EXPERTS, D_MODEL), D_MODEL ** -0.5)
    peer_v = nrm((DEPTH, N_EXPERTS, D_MODEL), N_RHEADS ** -0.5)
    return {
        'x_prompt': x_prompt, 'x_sample': x_sample,
        'state_rglru_h': state_rglru_h, 'state_rglru_conv': state_rglru_conv,
        'state_conformer_conv': state_conformer_conv,
        'norm_mix': norm_mix, 'norm_ffn': norm_ffn, 'norm_final': norm_final,
        'rg_w_in': rg_w_in, 'rg_conv_w': rg_conv_w, 'rg_conv_b': rg_conv_b,
        'rg_w_a': rg_w_a, 'rg_b_a': rg_b_a, 'rg_w_i': rg_w_i, 'rg_b_i': rg_b_i,
        'rg_lambda': rg_lambda, 'rg_w_out': rg_w_out,
        'cf_w_pw1': cf_w_pw1, 'cf_b_pw1': cf_b_pw1, 'cf_dw_w': cf_dw_w, 'cf_dw_b': cf_dw_b,
        'cf_ln_g': cf_ln_g, 'cf_ln_b': cf_ln_b, 'cf_w_pw2': cf_w_pw2, 'cf_b_pw2': cf_b_pw2,
        'peer_w_q': peer_w_q, 'peer_sub_keys': peer_sub_keys, 'peer_u': peer_u, 'peer_v': peer_v,
    }


def reference(x_prompt, x_sample, state_rglru_h, state_rglru_conv, state_conformer_conv,
              norm_mix, norm_ffn, norm_final, rg_w_in, rg_conv_w, rg_conv_b, rg_w_a, rg_b_a,
              rg_w_i, rg_b_i, rg_lambda, rg_w_out, cf_w_pw1, cf_b_pw1, cf_dw_w, cf_dw_b,
              cf_ln_g, cf_ln_b, cf_w_pw2, cf_b_pw2, peer_w_q, peer_sub_keys, peer_u, peer_v):
    params = dict(norm_mix=norm_mix, norm_ffn=norm_ffn, norm_final=norm_final,
                  rg_w_in=rg_w_in, rg_conv_w=rg_conv_w, rg_conv_b=rg_conv_b,
                  rg_w_a=rg_w_a, rg_b_a=rg_b_a, rg_w_i=rg_w_i, rg_b_i=rg_b_i,
                  rg_lambda=rg_lambda, rg_w_out=rg_w_out,
                  cf_w_pw1=cf_w_pw1, cf_b_pw1=cf_b_pw1, cf_dw_w=cf_dw_w, cf_dw_b=cf_dw_b,
                  cf_ln_g=cf_ln_g, cf_ln_b=cf_ln_b, cf_w_pw2=cf_w_pw2, cf_b_pw2=cf_b_pw2,
                  peer_w_q=peer_w_q, peer_sub_keys=peer_sub_keys, peer_u=peer_u, peer_v=peer_v)
    dt = x_prompt.dtype
    nbp = x_prompt.shape[0]
    h0_p = jnp.zeros((N_A_LAYERS, nbp, D_RNN), dt)
    rgc0_p = jnp.zeros((N_A_LAYERS, nbp, RG_CONV_W - 1, D_RNN), dt)
    cfc0_p = jnp.zeros((N_B_LAYERS, nbp, CONF_CONV_W - 1, D_CONV), dt)
    y_prompt, p_h, p_rgc, p_cfc = trunk(x_prompt, h0_p, rgc0_p, cfc0_p, params)
    y_sample, s_h, s_rgc, s_cfc = trunk(x_sample, state_rglru_h, state_rglru_conv,
                                        state_conformer_conv, params)
    return (y_prompt, y_sample, p_h, p_rgc, p_cfc, s_h, s_rgc, s_cfc)
```

```python
import functools

import jax
import jax.numpy as jnp
from jax import lax
from jax.experimental import pallas as pl
from jax.experimental.pallas import tpu as pltpu

F32 = jnp.float32
BF16 = jnp.bfloat16

EPS = 1e-6
RG_C = 8.0
TOPK = 16

LANES = 128
SUBLANES = 8
MXU_DIM = 256
VMEM_LIMIT = 56 * 1024 * 1024

ROW_TILE = 512
PEER_TOK_TILE = 512
PEER_EXP_TILE = 1024


def _rms(x, g):
    return x * lax.rsqrt(jnp.mean(x * x, axis=-1, keepdims=True) + EPS) * g


def _cparams(sem):
    return pltpu.CompilerParams(dimension_semantics=sem, vmem_limit_bytes=VMEM_LIMIT)


def _norm_matmul_kernel(x_ref, g_ref, w_ref, b_ref, o_ref):
    xn = _rms(x_ref[...], g_ref[...])
    o_ref[...] = jnp.dot(xn.astype(BF16), w_ref[...], preferred_element_type=F32) + b_ref[...]


def _norm_matmul(x, g, w, b):
    n, d = x.shape
    n_out = w.shape[1]
    tm = min(ROW_TILE, n)
    return pl.pallas_call(
        _norm_matmul_kernel,
        grid=(n // tm,),
        in_specs=[
            pl.BlockSpec((tm, d), lambda i: (i, 0)),
            pl.BlockSpec((1, d), lambda i: (0, 0)),
            pl.BlockSpec((d, n_out), lambda i: (0, 0)),
            pl.BlockSpec((1, n_out), lambda i: (0, 0)),
        ],
        out_specs=pl.BlockSpec((tm, n_out), lambda i: (i, 0)),
        out_shape=jax.ShapeDtypeStruct((n, n_out), F32),
        compiler_params=_cparams(("parallel",)),
        name="norm_matmul",
    )(x, g, w, b)


def _rg_kernel(x_ref, proj_ref, h0_ref, conv0_ref, cw_ref, cb_ref, wa_ref, ba_ref, wi_ref,
               bi_ref, lam_ref, wout_ref, o_ref, hlast_ref, cstate_ref,
               ext_ref, a_ref, u_ref, hcar_ref):
    tc, nb, d = x_ref.shape
    rows = tc * nb
    cw_taps = cw_ref.shape[0]
    hist = cw_taps - 1
    step = pl.program_id(1)

    @pl.when(step == 0)
    def _():
        ext_ref[0:hist] = conv0_ref[...]
        hcar_ref[...] = h0_ref[...]

    @pl.when(step > 0)
    def _():
        ext_ref[0:hist] = ext_ref[tc:tc + hist]

    ext_ref[hist:hist + tc] = proj_ref[:, :, d:]

    xc = cb_ref[...][None]
    for k in range(cw_taps):
        xc = xc + cw_ref[k:k + 1, :][None] * ext_ref[k:k + tc]
    xc2 = xc.reshape(rows, d)
    xcb = xc2.astype(BF16)

    def blockdiag(w_ref):
        n_grp = w_ref.shape[0]
        gw = d // n_grp
        outs = [jnp.dot(xcb[:, g * gw:(g + 1) * gw], w_ref[g], preferred_element_type=F32)
                for g in range(n_grp)]
        return jnp.concatenate(outs, axis=-1)

    r = jax.nn.sigmoid(blockdiag(wa_ref) + ba_ref[...])
    gi = jax.nn.sigmoid(blockdiag(wi_ref) + bi_ref[...])
    neg_lam = -lam_ref[...]
    softplus = jnp.maximum(neg_lam, 0.0) + jnp.log1p(jnp.exp(-jnp.abs(neg_lam)))
    log_a = (-RG_C) * r * softplus
    a = jnp.exp(log_a)
    u = jnp.sqrt(-jnp.tanh(log_a) * (a * a + 1.0)) * (gi * xc2)
    a_ref[...] = a.reshape(tc, nb, d)
    u_ref[...] = u.reshape(tc, nb, d)

    def scan_body(t, h):
        h = a_ref[t] * h + u_ref[t]
        u_ref[t] = h
        return h

    h_fin = lax.fori_loop(0, tc, scan_body, hcar_ref[...], unroll=min(tc, 8))
    hcar_ref[...] = h_fin

    gate = jax.nn.gelu(proj_ref[:, :, :d].reshape(rows, d))
    gated = (u_ref[...].reshape(rows, d) * gate).astype(BF16)
    y = jnp.dot(gated, wout_ref[...], preferred_element_type=F32)
    o_ref[...] = x_ref[...] + y.reshape(tc, nb, d)

    @pl.when(step == pl.num_programs(1) - 1)
    def _():
        hlast_ref[...] = h_fin
        cstate_ref[...] = ext_ref[tc:tc + hist]


def _seq_tiles(t, nb, max_bb):
    bb = min(nb, max_bb, max(SUBLANES, ROW_TILE // t))
    tc = min(t, ROW_TILE // bb)
    assert nb % bb == 0 and t % tc == 0 and bb % SUBLANES == 0
    return bb, tc


def _rg_block(x_tm, proj_tm, h0, conv0, cw, cb, wa_bd, ba, wi_bd, bi, lam, wout):
    t, nb, d = x_tm.shape
    bb, tc = _seq_tiles(t, nb, ROW_TILE)
    hist = cw.shape[0] - 1
    assert t == tc or tc >= hist
    const2 = lambda b, i: (0, 0)
    const3 = lambda b, i: (0, 0, 0)
    return pl.pallas_call(
        _rg_kernel,
        grid=(nb // bb, t // tc),
        in_specs=[
            pl.BlockSpec((tc, bb, d), lambda b, i: (i, b, 0)),
            pl.BlockSpec((tc, bb, 2 * d), lambda b, i: (i, b, 0)),
            pl.BlockSpec((bb, d), lambda b, i: (b, 0)),
            pl.BlockSpec((hist, bb, d), lambda b, i: (0, b, 0)),
            pl.BlockSpec(cw.shape, const2),
            pl.BlockSpec((1, d), const2),
            pl.BlockSpec(wa_bd.shape, const3),
            pl.BlockSpec((1, d), const2),
            pl.BlockSpec(wi_bd.shape, const3),
            pl.BlockSpec((1, d), const2),
            pl.BlockSpec((1, d), const2),
            pl.BlockSpec((d, d), const2),
        ],
        out_specs=[
            pl.BlockSpec((tc, bb, d), lambda b, i: (i, b, 0)),
            pl.BlockSpec((bb, d), lambda b, i: (b, 0)),
            pl.BlockSpec((hist, bb, d), lambda b, i: (0, b, 0)),
        ],
        out_shape=[
            jax.ShapeDtypeStruct((t, nb, d), F32),
            jax.ShapeDtypeStruct((nb, d), F32),
            jax.ShapeDtypeStruct((hist, nb, d), F32),
        ],
        scratch_shapes=[
            pltpu.VMEM((tc + hist, bb, d), F32),
            pltpu.VMEM((tc, bb, d), F32),
            pltpu.VMEM((tc, bb, d), F32),
            pltpu.VMEM((bb, d), F32),
        ],
        compiler_params=_cparams(("parallel", "arbitrary")),
        name="rg_block",
    )(x_tm, proj_tm, h0, conv0, cw, cb, wa_bd, ba, wi_bd, bi, lam, wout)


CONV_T_SUB = 4
CF_MAX_BATCH_ROWS = 32


def _cf_kernel(x_ref, p_ref, conv0_ref, dw_ref, dwb_ref, lng_ref, lnb_ref, w2_ref, b2_ref,
               o_ref, cstate_ref, ext_ref, c_ref, wb_ref):
    tc, nb, d = x_ref.shape
    rows = tc * nb
    taps = dw_ref.shape[0]
    hist = taps - 1
    step = pl.program_id(1)

    @pl.when(step == 0)
    def _():
        ext_ref[0:hist] = conv0_ref[...]
        for k in range(taps):
            wb_ref[k] = jnp.broadcast_to(dw_ref[k:k + 1, :], (SUBLANES, d))

    @pl.when(step > 0)
    def _():
        ext_ref[0:hist] = ext_ref[tc:tc + hist]

    ext_ref[hist:hist + tc] = p_ref[:, :, :d] * jax.nn.sigmoid(p_ref[:, :, d:])

    n_b = nb // SUBLANES
    bias = jnp.broadcast_to(dwb_ref[...], (SUBLANES, d))[None]

    def conv_body(idx, carry):
        t0 = (idx // n_b) * CONV_T_SUB
        b0 = pl.multiple_of((idx % n_b) * SUBLANES, SUBLANES)
        acc = jnp.broadcast_to(bias, (CONV_T_SUB, SUBLANES, d))
        for k in range(taps):
            acc = acc + wb_ref[k][None] * ext_ref[pl.ds(t0 + k, CONV_T_SUB), pl.ds(b0, SUBLANES), :]
        c_ref[pl.ds(t0, CONV_T_SUB), pl.ds(b0, SUBLANES), :] = acc
        return carry

    lax.fori_loop(0, (tc // CONV_T_SUB) * n_b, conv_body, 0)

    c = c_ref[...].reshape(rows, d)
    mu = jnp.mean(c, axis=-1, keepdims=True)
    cc = c - mu
    var = jnp.mean(cc * cc, axis=-1, keepdims=True)
    y = cc * lax.rsqrt(var + EPS) * lng_ref[...] + lnb_ref[...]
    y = jax.nn.silu(y).astype(BF16)
    out = jnp.dot(y, w2_ref[...], preferred_element_type=F32) + b2_ref[...]
    o_ref[...] = x_ref[...] + out.reshape(tc, nb, d)

    @pl.when(step == pl.num_programs(1) - 1)
    def _():
        cstate_ref[...] = ext_ref[tc:tc + hist]


def _cf_block(x_tm, p_tm, conv0, dw, dwb, lng, lnb, w2, b2):
    t, nb, d = x_tm.shape
    bb, tc = _seq_tiles(t, nb, CF_MAX_BATCH_ROWS)
    taps = dw.shape[0]
    hist = taps - 1
    assert tc % CONV_T_SUB == 0
    assert t == tc or tc >= hist
    const2 = lambda b, i: (0, 0)
    return pl.pallas_call(
        _cf_kernel,
        grid=(nb // bb, t // tc),
        in_specs=[
            pl.BlockSpec((tc, bb, d), lambda b, i: (i, b, 0)),
            pl.BlockSpec((tc, bb, 2 * d), lambda b, i: (i, b, 0)),
            pl.BlockSpec((hist, bb, d), lambda b, i: (0, b, 0)),
            pl.BlockSpec(dw.shape, const2),
            pl.BlockSpec((1, d), const2),
            pl.BlockSpec((1, d), const2),
            pl.BlockSpec((1, d), const2),
            pl.BlockSpec((d, d), const2),
            pl.BlockSpec((1, d), const2),
        ],
        out_specs=[
            pl.BlockSpec((tc, bb, d), lambda b, i: (i, b, 0)),
            pl.BlockSpec((hist, bb, d), lambda b, i: (0, b, 0)),
        ],
        out_shape=[
            jax.ShapeDtypeStruct((t, nb, d), F32),
            jax.ShapeDtypeStruct((hist, nb, d), F32),
        ],
        scratch_shapes=[
            pltpu.VMEM((tc + hist, bb, d), F32),
            pltpu.VMEM((tc, bb, d), F32),
            pltpu.VMEM((taps, SUBLANES, d), F32),
        ],
        compiler_params=_cparams(("parallel", "arbitrary")),
        name="cf_block",
    )(x_tm, p_tm, conv0, dw, dwb, lng, lnb, w2, b2)


def _sorted_topk(s, out_ref):
    work = s
    for i in range(TOPK):
        m = jnp.max(work, axis=0, keepdims=True)
        out_ref[i:i + 1, :] = m
        if i + 1 < TOPK:
            work = jnp.where(work >= m, -jnp.inf, work)


def _peer_stats_kernel(x_ref, g_ref, wqt_ref, keys_ref, xnt_ref, s2_ref, thr_ref, e1_ref,
                       e2n_ref, qt_ref, sv1_ref, sv2_ref):
    n_heads = s2_ref.shape[0]
    n_keys = s2_ref.shape[1]
    xn = _rms(x_ref[...], g_ref[...])
    xnt = xn.T.astype(BF16)
    xnt_ref[...] = xnt
    qt_ref[...] = jnp.dot(wqt_ref[...], xnt, preferred_element_type=F32)
    d_half = qt_ref.shape[0] // (2 * n_heads)

    def head_body(h, carry):
        q1 = qt_ref[pl.ds(pl.multiple_of(h * 2 * d_half, d_half), d_half), :]
        q2 = qt_ref[pl.ds(pl.multiple_of(h * 2 * d_half + d_half, d_half), d_half), :]
        s1 = jnp.dot(keys_ref[2 * h], q1, preferred_element_type=F32)
        s2 = jnp.dot(keys_ref[2 * h + 1], q2, preferred_element_type=F32)
        _sorted_topk(s1, sv1_ref)
        _sorted_topk(s2, sv2_ref)
        sv1 = sv1_ref[...]
        sv2 = sv2_ref[...]
        half = TOPK // 2
        cand = [sv1[0:1] + sv2]
        cand += [sv1[i:i + 1] + sv2[0:half] for i in range(1, half)]
        cand += [sv1[half:] + sv2[0:1]]
        work = jnp.concatenate(cand, axis=0)
        tau = None
        for i in range(TOPK):
            tau = jnp.max(work, axis=0, keepdims=True)
            if i + 1 < TOPK:
                work = jnp.where(work >= tau, -jnp.inf, work)
        e2v = jnp.exp(sv2 - sv2[0:1])
        thr_rows = jnp.full(sv1.shape, jnp.inf, F32)
        mass = jnp.zeros(sv1.shape, F32)
        prefix = jnp.zeros_like(tau)
        for j in range(TOPK):
            prefix = prefix + e2v[j:j + 1]
            cond = (sv1 + sv2[j:j + 1]) >= tau
            thr_rows = jnp.where(cond, sv2[j:j + 1], thr_rows)
            mass = jnp.where(cond, prefix, mass)
        z = jnp.sum(jnp.exp(sv1 - sv1[0:1]) * mass, axis=0, keepdims=True)
        thr = jnp.full(s1.shape, jnp.inf, F32)
        for i in range(TOPK):
            thr = jnp.where(s1 == sv1[i:i + 1], thr_rows[i:i + 1], thr)
        s2_ref[h] = s2
        thr_ref[h] = thr
        e1_ref[h] = jnp.exp(s1 - sv1[0:1])
        e2n_ref[h] = jnp.exp(s2 - sv2[0:1]) / z
        return carry

    lax.fori_loop(0, n_heads, head_body, 0)


def _peer_stats(x, g, wqt, keys):
    n, d = x.shape
    tt = min(PEER_TOK_TILE, n)
    n_heads = keys.shape[0] // 2
    n_keys = keys.shape[1]
    stat = jax.ShapeDtypeStruct((n_heads, n_keys, n), F32)
    stat_spec = pl.BlockSpec((n_heads, n_keys, tt), lambda i: (0, 0, i))
    return pl.pallas_call(
        _peer_stats_kernel,
        grid=(n // tt,),
        in_specs=[
            pl.BlockSpec((tt, d), lambda i: (i, 0)),
            pl.BlockSpec((1, d), lambda i: (0, 0)),
            pl.BlockSpec(wqt.shape, lambda i: (0, 0)),
            pl.BlockSpec(keys.shape, lambda i: (0, 0, 0)),
        ],
        out_specs=[pl.BlockSpec((d, tt), lambda i: (0, i)), stat_spec, stat_spec, stat_spec,
                   stat_spec],
        out_shape=[jax.ShapeDtypeStruct((d, n), BF16), stat, stat, stat, stat],
        scratch_shapes=[
            pltpu.VMEM((wqt.shape[0], tt), F32),
            pltpu.VMEM((TOPK, tt), F32),
            pltpu.VMEM((TOPK, tt), F32),
        ],
        compiler_params=_cparams(("parallel",)),
        name="peer_stats",
    )(x, g, wqt, keys)


def _peer_dense_kernel(x_ref, xnt_ref, s2_ref, thr_ref, e1_ref, e2n_ref, u_ref, vt_ref, gf_ref,
                       o_ref, acc_ref, a_ref, w_ref, thrb_ref, e1b_ref, *, final_norm):
    n_heads, n_keys, tt = s2_ref.shape
    et = u_ref.shape[0]
    n_k1 = et // n_keys
    j = pl.program_id(1)

    @pl.when(j == 0)
    def _():
        acc_ref[...] = jnp.zeros_like(acc_ref)

    a_ref[...] = jnp.dot(u_ref[...], xnt_ref[...], preferred_element_type=F32)

    for h in range(n_heads):
        thr_h = thr_ref[h]
        e1_h = e1_ref[h]
        for kk in range(n_k1):
            thrb_ref[h, kk] = jnp.broadcast_to(thr_h[kk:kk + 1, :], (SUBLANES, tt))
            e1b_ref[h, kk] = jnp.broadcast_to(e1_h[kk:kk + 1, :], (SUBLANES, tt))

    def k1_body(kk, carry):
        row0 = pl.multiple_of(kk * n_keys, n_keys)
        for c in range(tt // LANES):
            cols = pl.ds(c * LANES, LANES)
            g = jnp.zeros((n_keys // SUBLANES, SUBLANES, LANES), F32)
            for h in range(n_heads):
                thr = thrb_ref[h, kk, :, cols][None]
                e1 = e1b_ref[h, kk, :, cols][None]
                s2 = s2_ref[h, :, cols].reshape(g.shape)
                e2n = e2n_ref[h, :, cols].reshape(g.shape)
                g = g + jnp.where(s2 >= thr, e2n, 0.0) * e1
            act = jax.nn.gelu(a_ref[pl.ds(row0, n_keys), cols])
            w_ref[pl.ds(row0, n_keys), cols] = (g.reshape(n_keys, LANES) * act).astype(BF16)
        return carry

    lax.fori_loop(0, n_k1, k1_body, 0)

    acc_ref[...] += jnp.dot(vt_ref[...], w_ref[...], preferred_element_type=F32)

    @pl.when(j == pl.num_programs(1) - 1)
    def _():
        y = x_ref[...] + acc_ref[...].T
        if final_norm:
            y = _rms(y, gf_ref[...])
        o_ref[...] = y


def _peer_dense(x, xnt, s2, thr, e1, e2n, u_bf, vt_bf, g_final, final_norm):
    n, d = x.shape
    tt = min(PEER_TOK_TILE, n)
    n_heads, n_keys, _ = s2.shape
    n_exp = u_bf.shape[0]
    et = PEER_EXP_TILE
    n_k1 = et // n_keys
    stat_spec = pl.BlockSpec((n_heads, n_keys, tt), lambda i, j: (0, 0, i))
    k1_spec = pl.BlockSpec((n_heads, n_k1, tt), lambda i, j: (0, j, i))
    return pl.pallas_call(
        functools.partial(_peer_dense_kernel, final_norm=final_norm),
        grid=(n // tt, n_exp // et),
        in_specs=[
            pl.BlockSpec((tt, d), lambda i, j: (i, 0)),
            pl.BlockSpec((d, tt), lambda i, j: (0, i)),
            stat_spec, k1_spec, k1_spec, stat_spec,
            pl.BlockSpec((et, d), lambda i, j: (j, 0)),
            pl.BlockSpec((d, et), lambda i, j: (0, j)),
            pl.BlockSpec((1, d), lambda i, j: (0, 0)),
        ],
        out_specs=pl.BlockSpec((tt, d), lambda i, j: (i, 0)),
        out_shape=jax.ShapeDtypeStruct((n, d), F32),
        scratch_shapes=[
            pltpu.VMEM((d, tt), F32),
            pltpu.VMEM((et, tt), F32),
            pltpu.VMEM((et, tt), BF16),
            pltpu.VMEM((n_heads, n_k1, SUBLANES, tt), F32),
            pltpu.VMEM((n_heads, n_k1, SUBLANES, tt), F32),
        ],
        compiler_params=_cparams(("parallel", "arbitrary")),
        name="peer_dense",
    )(x, xnt, s2, thr, e1, e2n, u_bf, vt_bf, g_final)


def _peer(x, g, wqt, keys, u_bf, vt_bf, g_final, final_norm):
    xnt, s2, thr, e1, e2n = _peer_stats(x, g, wqt, keys)
    return _peer_dense(x, xnt, s2, thr, e1, e2n, u_bf, vt_bf, g_final, final_norm)


def _block_diag_groups(w):
    n_blocks, bs, _ = w.shape
    per = MXU_DIM // bs
    w4 = w.reshape(n_blocks // per, per, bs, bs)
    eye = jnp.eye(per, dtype=w.dtype)
    bd = jnp.einsum('gaij,ab->gaibj', w4, eye)
    return bd.reshape(n_blocks // per, MXU_DIM, MXU_DIM).astype(BF16)


def _prep_weights(p):
    row = lambda v: v.reshape(1, -1)
    depth = p['norm_mix'].shape[0]
    w = dict(p)
    w['rg_w_in'] = p['rg_w_in'][0].astype(BF16)
    w['rg_wa_bd'] = _block_diag_groups(p['rg_w_a'][0])
    w['rg_wi_bd'] = _block_diag_groups(p['rg_w_i'][0])
    w['rg_w_out'] = p['rg_w_out'][0].astype(BF16)
    w['cf_w_pw1'] = p['cf_w_pw1'][0].astype(BF16)
    w['cf_w_pw2'] = p['cf_w_pw2'][0].astype(BF16)
    w['peer_wqt'] = [p['peer_w_q'][i].T.astype(BF16) for i in range(depth)]
    w['peer_keys'] = [p['peer_sub_keys'][i].reshape((-1,) + p['peer_sub_keys'].shape[-2:])
                      for i in range(depth)]
    w['peer_u_bf'] = [p['peer_u'][i].astype(BF16) for i in range(depth)]
    w['peer_vt_bf'] = [p['peer_v'][i].T.astype(BF16) for i in range(depth)]
    w['row'] = row
    return w


def _trunk(x_btd, rg_h0, rg_conv0, cf_conv0, w):
    nb, t, d = x_btd.shape
    row = w['row']
    x = jnp.transpose(x_btd, (1, 0, 2))
    n = t * nb
    zeros2d = jnp.zeros((1, 2 * d), F32)

    proj = _norm_matmul(x.reshape(n, d), row(w['norm_mix'][0]), w['rg_w_in'], zeros2d)
    x, h_last, rg_cs = _rg_block(
        x, proj.reshape(t, nb, 2 * d), rg_h0[0], jnp.transpose(rg_conv0[0], (1, 0, 2)),
        w['rg_conv_w'][0], row(w['rg_conv_b'][0]), w['rg_wa_bd'], row(w['rg_b_a'][0]),
        w['rg_wi_bd'], row(w['rg_b_i'][0]), row(w['rg_lambda'][0]), w['rg_w_out'])
    x = _peer(x.reshape(n, d), row(w['norm_ffn'][0]), w['peer_wqt'][0], w['peer_keys'][0],
              w['peer_u_bf'][0], w['peer_vt_bf'][0], row(w['norm_final']), False)

    pp = _norm_matmul(x, row(w['norm_mix'][1]), w['cf_w_pw1'], row(w['cf_b_pw1'][0]))
    x, cf_cs = _cf_block(
        x.reshape(t, nb, d), pp.reshape(t, nb, 2 * d), jnp.transpose(cf_conv0[0], (1, 0, 2)),
        w['cf_dw_w'][0], row(w['cf_dw_b'][0]), row(w['cf_ln_g'][0]), row(w['cf_ln_b'][0]),
        w['cf_w_pw2'], row(w['cf_b_pw2'][0]))
    y = _peer(x.reshape(n, d), row(w['norm_ffn'][1]), w['peer_wqt'][1], w['peer_keys'][1],
              w['peer_u_bf'][1], w['peer_vt_bf'][1], row(w['norm_final']), True)

    y = jnp.transpose(y.reshape(t, nb, d), (1, 0, 2))
    return (y, h_last[None], jnp.transpose(rg_cs, (1, 0, 2))[None],
            jnp.transpose(cf_cs, (1, 0, 2))[None])


def kernel(x_prompt, x_sample, state_rglru_h, state_rglru_conv, state_conformer_conv, norm_mix, norm_ffn, norm_final, rg_w_in, rg_conv_w, rg_conv_b, rg_w_a, rg_b_a, rg_w_i, rg_b_i, rg_lambda, rg_w_out, cf_w_pw1, cf_b_pw1, cf_dw_w, cf_dw_b, cf_ln_g, cf_ln_b, cf_w_pw2, cf_b_pw2, peer_w_q, peer_sub_keys, peer_u, peer_v):
    assert norm_mix.shape[0] == 2 and rg_w_in.shape[0] == 1 and cf_w_pw1.shape[0] == 1
    params = dict(norm_mix=norm_mix, norm_ffn=norm_ffn, norm_final=norm_final,
                  rg_w_in=rg_w_in, rg_conv_w=rg_conv_w, rg_conv_b=rg_conv_b,
                  rg_w_a=rg_w_a, rg_b_a=rg_b_a, rg_w_i=rg_w_i, rg_b_i=rg_b_i,
                  rg_lambda=rg_lambda, rg_w_out=rg_w_out,
                  cf_w_pw1=cf_w_pw1, cf_b_pw1=cf_b_pw1, cf_dw_w=cf_dw_w, cf_dw_b=cf_dw_b,
                  cf_ln_g=cf_ln_g, cf_ln_b=cf_ln_b, cf_w_pw2=cf_w_pw2, cf_b_pw2=cf_b_pw2,
                  peer_w_q=peer_w_q, peer_sub_keys=peer_sub_keys, peer_u=peer_u, peer_v=peer_v)
    w = _prep_weights(params)
    dt = x_prompt.dtype
    nbp, _, d = x_prompt.shape
    n_a, n_b = state_rglru_h.shape[0], state_conformer_conv.shape[0]
    h0_p = jnp.zeros((n_a, nbp, d), dt)
    rgc0_p = jnp.zeros((n_a, nbp) + state_rglru_conv.shape[2:], dt)
    cfc0_p = jnp.zeros((n_b, nbp) + state_conformer_conv.shape[2:], dt)
    y_p, p_h, p_rgc, p_cfc = _trunk(x_prompt, h0_p, rgc0_p, cfc0_p, w)
    y_s, s_h, s_rgc, s_cfc = _trunk(x_sample, state_rglru_h, state_rglru_conv,
                                    state_conformer_conv, w)
    return (y_p, y_s, p_h, p_rgc, p_cfc, s_h, s_rgc, s_cfc)
```

```python
import functools

import jax
import jax.numpy as jnp
from jax import lax
from jax.experimental import pallas as pl
from jax.experimental.pallas import tpu as pltpu

F32 = jnp.float32
BF16 = jnp.bfloat16

EPS = 1e-6
RG_C = 8.0
TOPK = 16

LANES = 128
SUBLANES = 8
MXU_DIM = 256
VMEM_LIMIT = 56 * 1024 * 1024

ROW_TILE = 512
PEER_TOK_TILE = 512
PEER_EXP_TILE = 1024


def _rms(x, g):
    return x * lax.rsqrt(jnp.mean(x * x, axis=-1, keepdims=True) + EPS) * g


def _cparams(sem):
    return pltpu.CompilerParams(dimension_semantics=sem, vmem_limit_bytes=VMEM_LIMIT)


def _norm_matmul_kernel(x_ref, g_ref, w_ref, b_ref, o_ref):
    xn = _rms(x_ref[...], g_ref[...])
    o_ref[...] = jnp.dot(xn.astype(BF16), w_ref[...], preferred_element_type=F32) + b_ref[...]


def _norm_matmul(x, g, w, b):
    n, d = x.shape
    n_out = w.shape[1]
    tm = min(ROW_TILE, n)
    return pl.pallas_call(
        _norm_matmul_kernel,
        grid=(n // tm,),
        in_specs=[
            pl.BlockSpec((tm, d), lambda i: (i, 0)),
            pl.BlockSpec((1, d), lambda i: (0, 0)),
            pl.BlockSpec((d, n_out), lambda i: (0, 0)),
            pl.BlockSpec((1, n_out), lambda i: (0, 0)),
        ],
        out_specs=pl.BlockSpec((tm, n_out), lambda i: (i, 0)),
        out_shape=jax.ShapeDtypeStruct((n, n_out), F32),
        compiler_params=_cparams(("parallel",)),
        name="norm_matmul",
    )(x, g, w, b)


def _rg_kernel(x_ref, proj_ref, h0_ref, conv0_ref, cw_ref, cb_ref, wa_ref, ba_ref, wi_ref,
               bi_ref, lam_ref, wout_ref, o_ref, hlast_ref, cstate_ref,
               ext_ref, a_ref, u_ref, hcar_ref):
    tc, nb, d = x_ref.shape
    rows = tc * nb
    cw_taps = cw_ref.shape[0]
    hist = cw_taps - 1
    step = pl.program_id(1)

    @pl.when(step == 0)
    def _():
        ext_ref[0:hist] = conv0_ref[...]
        hcar_ref[...] = h0_ref[...]

    @pl.when(step > 0)
    def _():
        ext_ref[0:hist] = ext_ref[tc:tc + hist]

    ext_ref[hist:hist + tc] = proj_ref[:, :, d:]

    xc = cb_ref[...][None]
    for k in range(cw_taps):
        xc = xc + cw_ref[k:k + 1, :][None] * ext_ref[k:k + tc]
    xc2 = xc.reshape(rows, d)
    xcb = xc2.astype(BF16)

    def blockdiag(w_ref):
        n_grp = w_ref.shape[0]
        gw = d // n_grp
        outs = [jnp.dot(xcb[:, g * gw:(g + 1) * gw], w_ref[g], preferred_element_type=F32)
                for g in range(n_grp)]
        return jnp.concatenate(outs, axis=-1)

    r = jax.nn.sigmoid(blockdiag(wa_ref) + ba_ref[...])
    gi = jax.nn.sigmoid(blockdiag(wi_ref) + bi_ref[...])
    neg_lam = -lam_ref[...]
    softplus = jnp.maximum(neg_lam, 0.0) + jnp.log1p(jnp.exp(-jnp.abs(neg_lam)))
    log_a = (-RG_C) * r * softplus
    a = jnp.exp(log_a)
    u = jnp.sqrt(-jnp.tanh(log_a) * (a * a + 1.0)) * (gi * xc2)
    a_ref[...] = a.reshape(tc, nb, d)
    u_ref[...] = u.reshape(tc, nb, d)

    def scan_body(t, h):
        h = a_ref[t] * h + u_ref[t]
        u_ref[t] = h
        return h

    h_fin = lax.fori_loop(0, tc, scan_body, hcar_ref[...], unroll=min(tc, 8))
    hcar_ref[...] = h_fin

    gate = jax.nn.gelu(proj_ref[:, :, :d].reshape(rows, d))
    gated = (u_ref[...].reshape(rows, d) * gate).astype(BF16)
    y = jnp.dot(gated, wout_ref[...], preferred_element_type=F32)
    o_ref[...] = x_ref[...] + y.reshape(tc, nb, d)

    @pl.when(step == pl.num_programs(1) - 1)
    def _():
        hlast_ref[...] = h_fin
        cstate_ref[...] = ext_ref[tc:tc + hist]


def _seq_tiles(t, nb, max_bb):
    bb = min(nb, max_bb, max(SUBLANES, ROW_TILE // t))
    tc = min(t, ROW_TILE // bb)
    assert nb % bb == 0 and t % tc == 0 and bb % SUBLANES == 0
    return bb, tc


def _rg_block(x_tm, proj_tm, h0, conv0, cw, cb, wa_bd, ba, wi_bd, bi, lam, wout):
    t, nb, d = x_tm.shape
    bb, tc = _seq_tiles(t, nb, ROW_TILE)
    hist = cw.shape[0] - 1
    assert t == tc or tc >= hist
    const2 = lambda b, i: (0, 0)
    const3 = lambda b, i: (0, 0, 0)
    return pl.pallas_call(
        _rg_kernel,
        grid=(nb // bb, t // tc),
        in_specs=[
            pl.BlockSpec((tc, bb, d), lambda b, i: (i, b, 0)),
            pl.BlockSpec((tc, bb, 2 * d), lambda b, i: (i, b, 0)),
            pl.BlockSpec((bb, d), lambda b, i: (b, 0)),
            pl.BlockSpec((hist, bb, d), lambda b, i: (0, b, 0)),
            pl.BlockSpec(cw.shape, const2),
            pl.BlockSpec((1, d), const2),
            pl.BlockSpec(wa_bd.shape, const3),
            pl.BlockSpec((1, d), const2),
            pl.BlockSpec(wi_bd.shape, const3),
            pl.BlockSpec((1, d), const2),
            pl.BlockSpec((1, d), const2),
            pl.BlockSpec((d, d), const2),
        ],
        out_specs=[
            pl.BlockSpec((tc, bb, d), lambda b, i: (i, b, 0)),
            pl.BlockSpec((bb, d), lambda b, i: (b, 0)),
            pl.BlockSpec((hist, bb, d), lambda b, i: (0, b, 0)),
        ],
        out_shape=[
            jax.ShapeDtypeStruct((t, nb, d), F32),
            jax.ShapeDtypeStruct((nb, d), F32),
            jax.ShapeDtypeStruct((hist, nb, d), F32),
        ],
        scratch_shapes=[
            pltpu.VMEM((tc + hist, bb, d), F32),
            pltpu.VMEM((tc, bb, d), F32),
            pltpu.VMEM((tc, bb, d), F32),
            pltpu.VMEM((bb, d), F32),
        ],
        compiler_params=_cparams(("parallel", "arbitrary")),
        name="rg_block",
    )(x_tm, proj_tm, h0, conv0, cw, cb, wa_bd, ba, wi_bd, bi, lam, wout)


CONV_T_SUB = 4
CF_MAX_BATCH_ROWS = 32


def _cf_kernel(x_ref, p_ref, conv0_ref, dw_ref, dwb_ref, lng_ref, lnb_ref, w2_ref, b2_ref,
               o_ref, cstate_ref, ext_ref, c_ref, wb_ref):
    tc, nb, d = x_ref.shape
    rows = tc * nb
    taps = dw_ref.shape[0]
    hist = taps - 1
    step = pl.program_id(1)

    @pl.when(step == 0)
    def _():
        ext_ref[0:hist] = conv0_ref[...]
        for k in range(taps):
            wb_ref[k] = jnp.broadcast_to(dw_ref[k:k + 1, :], (SUBLANES, d))

    @pl.when(step > 0)
    def _():
        ext_ref[0:hist] = ext_ref[tc:tc + hist]

    ext_ref[hist:hist + tc] = p_ref[:, :, :d] * jax.nn.sigmoid(p_ref[:, :, d:])

    n_b = nb // SUBLANES
    bias = jnp.broadcast_to(dwb_ref[...], (SUBLANES, d))[None]

    def conv_body(idx, carry):
        t0 = (idx // n_b) * CONV_T_SUB
        b0 = pl.multiple_of((idx % n_b) * SUBLANES, SUBLANES)
        acc = jnp.broadcast_to(bias, (CONV_T_SUB, SUBLANES, d))
        for k in range(taps):
            acc = acc + wb_ref[k][None] * ext_ref[pl.ds(t0 + k, CONV_T_SUB), pl.ds(b0, SUBLANES), :]
        c_ref[pl.ds(t0, CONV_T_SUB), pl.ds(b0, SUBLANES), :] = acc
        return carry

    lax.fori_loop(0, (tc // CONV_T_SUB) * n_b, conv_body, 0)

    c = c_ref[...].reshape(rows, d)
    mu = jnp.mean(c, axis=-1, keepdims=True)
    cc = c - mu
    var = jnp.mean(cc * cc, axis=-1, keepdims=True)
    y = cc * lax.rsqrt(var + EPS) * lng_ref[...] + lnb_ref[...]
    y = jax.nn.silu(y).astype(BF16)
    out = jnp.dot(y, w2_ref[...], preferred_element_type=F32) + b2_ref[...]
    o_ref[...] = x_ref[...] + out.reshape(tc, nb, d)

    @pl.when(step == pl.num_programs(1) - 1)
    def _():
        cstate_ref[...] = ext_ref[tc:tc + hist]


def _cf_block(x_tm, p_tm, conv0, dw, dwb, lng, lnb, w2, b2):
    t, nb, d = x_tm.shape
    bb, tc = _seq_tiles(t, nb, CF_MAX_BATCH_ROWS)
    taps = dw.shape[0]
    hist = taps - 1
    assert tc % CONV_T_SUB == 0
    assert t == tc or tc >= hist
    const2 = lambda b, i: (0, 0)
    return pl.pallas_call(
        _cf_kernel,
        grid=(nb // bb, t // tc),
        in_specs=[
            pl.BlockSpec((tc, bb, d), lambda b, i: (i, b, 0)),
            pl.BlockSpec((tc, bb, 2 * d), lambda b, i: (i, b, 0)),
            pl.BlockSpec((hist, bb, d), lambda b, i: (0, b, 0)),
            pl.BlockSpec(dw.shape, const2),
            pl.BlockSpec((1, d), const2),
            pl.BlockSpec((1, d), const2),
            pl.BlockSpec((1, d), const2),
            pl.BlockSpec((d, d), const2),
            pl.BlockSpec((1, d), const2),
        ],
        out_specs=[
            pl.BlockSpec((tc, bb, d), lambda b, i: (i, b, 0)),
            pl.BlockSpec((hist, bb, d), lambda b, i: (0, b, 0)),
        ],
        out_shape=[
            jax.ShapeDtypeStruct((t, nb, d), F32),
            jax.ShapeDtypeStruct((hist, nb, d), F32),
        ],
        scratch_shapes=[
            pltpu.VMEM((tc + hist, bb, d), F32),
            pltpu.VMEM((tc, bb, d), F32),
            pltpu.VMEM((taps, SUBLANES, d), F32),
        ],
        compiler_params=_cparams(("parallel", "arbitrary")),
        name="cf_block",
    )(x_tm, p_tm, conv0, dw, dwb, lng, lnb, w2, b2)


def _sorted_topk(s, sv_ref, want_rank):
    work = s
    rank = jnp.full(s.shape, float(TOPK), F32) if want_rank else None
    for i in range(TOPK):
        m = jnp.max(work, axis=0, keepdims=True)
        sv_ref[i:i + 1, :] = m
        hit = work >= m
        if want_rank:
            rank = jnp.where(hit, float(i), rank)
        if i + 1 < TOPK:
            work = jnp.where(hit, -jnp.inf, work)
    return rank


def _peer_stats_kernel(x_ref, g_ref, wqt_ref, keys_ref, xnt_ref, rank2_ref, cnt1_ref, e1_ref,
                       e2n_ref, qt_ref, sv1_ref, sv2_ref):
    n_heads = rank2_ref.shape[0]
    xn = _rms(x_ref[...], g_ref[...])
    xnt = xn.T.astype(BF16)
    xnt_ref[...] = xnt
    qt_ref[...] = jnp.dot(wqt_ref[...], xnt, preferred_element_type=F32)
    d_half = qt_ref.shape[0] // (2 * n_heads)

    def head_body(h, carry):
        q1 = qt_ref[pl.ds(pl.multiple_of(h * 2 * d_half, d_half), d_half), :]
        q2 = qt_ref[pl.ds(pl.multiple_of(h * 2 * d_half + d_half, d_half), d_half), :]
        s1 = jnp.dot(keys_ref[2 * h], q1, preferred_element_type=F32)
        s2 = jnp.dot(keys_ref[2 * h + 1], q2, preferred_element_type=F32)
        _sorted_topk(s1, sv1_ref, False)
        rank2 = _sorted_topk(s2, sv2_ref, True)
        sv1 = sv1_ref[...]
        sv2 = sv2_ref[...]
        half = TOPK // 2
        cand = [sv1[0:1] + sv2]
        cand += [sv1[i:i + 1] + sv2[0:half] for i in range(1, half)]
        cand += [sv1[half:] + sv2[0:1]]
        work = jnp.concatenate(cand, axis=0)
        tau = None
        for i in range(TOPK):
            tau = jnp.max(work, axis=0, keepdims=True)
            if i + 1 < TOPK:
                work = jnp.where(work >= tau, -jnp.inf, work)
        e2v = jnp.exp(sv2 - sv2[0:1])
        cnt_rows = jnp.zeros(sv1.shape, F32)
        mass = jnp.zeros(sv1.shape, F32)
        prefix = jnp.zeros_like(tau)
        for j in range(TOPK):
            prefix = prefix + e2v[j:j + 1]
            cond = (sv1 + sv2[j:j + 1]) >= tau
            cnt_rows = cnt_rows + jnp.where(cond, 1.0, 0.0)
            mass = jnp.where(cond, prefix, mass)
        z = jnp.sum(jnp.exp(sv1 - sv1[0:1]) * mass, axis=0, keepdims=True)
        cnt1 = jnp.zeros(s1.shape, F32)
        for i in range(TOPK):
            cnt1 = jnp.where(s1 == sv1[i:i + 1], cnt_rows[i:i + 1], cnt1)
        rank2_ref[h] = rank2.astype(BF16)
        cnt1_ref[h] = cnt1
        e1_ref[h] = jnp.exp(s1 - sv1[0:1])
        e2n_ref[h] = (jnp.exp(s2 - sv2[0:1]) / z).astype(BF16)
        return carry

    lax.fori_loop(0, n_heads, head_body, 0)


def _peer_stats(x, g, wqt, keys):
    n, d = x.shape
    tt = min(PEER_TOK_TILE, n)
    n_heads = keys.shape[0] // 2
    n_keys = keys.shape[1]
    stat_spec = pl.BlockSpec((n_heads, n_keys, tt), lambda i: (0, 0, i))
    stat_f32 = jax.ShapeDtypeStruct((n_heads, n_keys, n), F32)
    stat_bf16 = jax.ShapeDtypeStruct((n_heads, n_keys, n), BF16)
    return pl.pallas_call(
        _peer_stats_kernel,
        grid=(n // tt,),
        in_specs=[
            pl.BlockSpec((tt, d), lambda i: (i, 0)),
            pl.BlockSpec((1, d), lambda i: (0, 0)),
            pl.BlockSpec(wqt.shape, lambda i: (0, 0)),
            pl.BlockSpec(keys.shape, lambda i: (0, 0, 0)),
        ],
        out_specs=[pl.BlockSpec((d, tt), lambda i: (0, i)), stat_spec, stat_spec, stat_spec,
                   stat_spec],
        out_shape=[jax.ShapeDtypeStruct((d, n), BF16), stat_bf16, stat_f32, stat_f32, stat_bf16],
        scratch_shapes=[
            pltpu.VMEM((wqt.shape[0], tt), F32),
            pltpu.VMEM((TOPK, tt), F32),
            pltpu.VMEM((TOPK, tt), F32),
        ],
        compiler_params=_cparams(("parallel",)),
        name="peer_stats",
    )(x, g, wqt, keys)


BF16_ROWS = 16


def _peer_dense_kernel(x_ref, xnt_ref, rank2_ref, cnt1_ref, e1_ref, e2n_ref, u_ref, vt_ref,
                       gf_ref, o_ref, acc_ref, a_ref, w_ref, xn_s, r2_s, e2_s, *, final_norm):
    n_heads, n_keys, tt = rank2_ref.shape
    et = u_ref.shape[0]
    n_k1 = et // n_keys
    j = pl.program_id(1)
    n_j = pl.num_programs(1) - 2

    def activation_matmul(slot):
        a_ref[slot] = jnp.dot(u_ref[...], xn_s[...], preferred_element_type=F32)

    def value_matmul(slot):
        return jnp.dot(vt_ref[...], w_ref[slot], preferred_element_type=F32)

    def build_weighted(slot):
        for kk in range(n_k1):
            rows = slice(kk * n_keys, (kk + 1) * n_keys)
            for c in range(tt // LANES):
                cols = slice(c * LANES, (c + 1) * LANES)
                g = jnp.zeros((n_keys, LANES), BF16)
                for h in range(n_heads):
                    cnt = jnp.broadcast_to(cnt1_ref[h, kk:kk + 1, cols], (n_keys, LANES))
                    e1 = jnp.broadcast_to(e1_ref[h, kk:kk + 1, cols], (n_keys, LANES))
                    r2 = r2_s[h, :, cols]
                    e2 = e2_s[h, :, cols]
                    g = g + jnp.where(r2 < cnt.astype(BF16), e2, jnp.zeros_like(e2)) * e1.astype(BF16)
                act = jax.nn.gelu(a_ref[slot, rows, cols]).astype(BF16)
                w_ref[slot, rows, cols] = g * act

    @pl.when(j == 0)
    def _():
        acc_ref[...] = jnp.zeros_like(acc_ref)
        w_ref[...] = jnp.zeros_like(w_ref)
        xn_s[...] = xnt_ref[...]
        r2_s[...] = rank2_ref[...]
        e2_s[...] = e2n_ref[...]
        activation_matmul(0)

    @pl.when((j >= 1) & (j <= n_j))
    def _():
        cur = j % 2
        activation_matmul(cur)
        build_weighted(1 - cur)
        acc_ref[...] += value_matmul(cur)

    @pl.when(j == n_j + 1)
    def _():
        y = x_ref[...] + (acc_ref[...] + value_matmul(j % 2)).T
        if final_norm:
            y = _rms(y, gf_ref[...])
        o_ref[...] = y


def _peer_dense(x, xnt, rank2, cnt1, e1, e2n, u_bf, vt_bf, g_final, final_norm):
    n, d = x.shape
    tt = min(PEER_TOK_TILE, n)
    n_heads, n_keys, _ = rank2.shape
    et = PEER_EXP_TILE
    n_j = u_bf.shape[0] // et
    n_k1 = et // n_keys
    last = n_j - 1
    stat_spec = pl.BlockSpec((n_heads, n_keys, tt), lambda i, j: (0, 0, i))
    k1_spec = pl.BlockSpec((n_heads, n_k1, tt),
                           lambda i, j: (0, jnp.clip(j - 1, 0, last), i))
    return pl.pallas_call(
        functools.partial(_peer_dense_kernel, final_norm=final_norm),
        grid=(n // tt, n_j + 2),
        in_specs=[
            pl.BlockSpec((tt, d), lambda i, j: (i, 0)),
            pl.BlockSpec((d, tt), lambda i, j: (0, i)),
            stat_spec, k1_spec, k1_spec, stat_spec,
            pl.BlockSpec((et, d), lambda i, j: (jnp.minimum(j, last), 0)),
            pl.BlockSpec((d, et), lambda i, j: (0, jnp.clip(j - 2, 0, last))),
            pl.BlockSpec((1, d), lambda i, j: (0, 0)),
        ],
        out_specs=pl.BlockSpec((tt, d), lambda i, j: (i, 0)),
        out_shape=jax.ShapeDtypeStruct((n, d), F32),
        scratch_shapes=[
            pltpu.VMEM((d, tt), F32),
            pltpu.VMEM((2, et, tt), F32),
            pltpu.VMEM((2, et, tt), BF16),
            pltpu.VMEM((d, tt), BF16),
            pltpu.VMEM((n_heads, n_keys, tt), BF16),
            pltpu.VMEM((n_heads, n_keys, tt), BF16),
        ],
        compiler_params=_cparams(("parallel", "arbitrary")),
        name="peer_dense",
    )(x, xnt, rank2, cnt1, e1, e2n, u_bf, vt_bf, g_final)


def _peer(x, g, wqt, keys, u_bf, vt_bf, g_final, final_norm):
    xnt, rank2, cnt1, e1, e2n = _peer_stats(x, g, wqt, keys)
    return _peer_dense(x, xnt, rank2, cnt1, e1, e2n, u_bf, vt_bf, g_final, final_norm)


def _block_diag_groups(w):
    n_blocks, bs, _ = w.shape
    per = MXU_DIM // bs
    w4 = w.reshape(n_blocks // per, per, bs, bs)
    eye = jnp.eye(per, dtype=w.dtype)
    bd = jnp.einsum('gaij,ab->gaibj', w4, eye)
    return bd.reshape(n_blocks // per, MXU_DIM, MXU_DIM).astype(BF16)


def _prep_weights(p):
    row = lambda v: v.reshape(1, -1)
    depth = p['norm_mix'].shape[0]
    w = dict(p)
    w['rg_w_in'] = p['rg_w_in'][0].astype(BF16)
    w['rg_wa_bd'] = _block_diag_groups(p['rg_w_a'][0])
    w['rg_wi_bd'] = _block_diag_groups(p['rg_w_i'][0])
    w['rg_w_out'] = p['rg_w_out'][0].astype(BF16)
    w['cf_w_pw1'] = p['cf_w_pw1'][0].astype(BF16)
    w['cf_w_pw2'] = p['cf_w_pw2'][0].astype(BF16)
    w['peer_wqt'] = [p['peer_w_q'][i].T.astype(BF16) for i in range(depth)]
    w['peer_keys'] = [p['peer_sub_keys'][i].reshape((-1,) + p['peer_sub_keys'].shape[-2:])
                      for i in range(depth)]
    w['peer_u_bf'] = [p['peer_u'][i].astype(BF16) for i in range(depth)]
    w['peer_vt_bf'] = [p['peer_v'][i].T.astype(BF16) for i in range(depth)]
    w['row'] = row
    return w


def _trunk(x_btd, rg_h0, rg_conv0, cf_conv0, w):
    nb, t, d = x_btd.shape
    row = w['row']
    x = jnp.transpose(x_btd, (1, 0, 2))
    n = t * nb
    zeros2d = jnp.zeros((1, 2 * d), F32)

    proj = _norm_matmul(x.reshape(n, d), row(w['norm_mix'][0]), w['rg_w_in'], zeros2d)
    x, h_last, rg_cs = _rg_block(
        x, proj.reshape(t, nb, 2 * d), rg_h0[0], jnp.transpose(rg_conv0[0], (1, 0, 2)),
        w['rg_conv_w'][0], row(w['rg_conv_b'][0]), w['rg_wa_bd'], row(w['rg_b_a'][0]),
        w['rg_wi_bd'], row(w['rg_b_i'][0]), row(w['rg_lambda'][0]), w['rg_w_out'])
    x = _peer(x.reshape(n, d), row(w['norm_ffn'][0]), w['peer_wqt'][0], w['peer_keys'][0],
              w['peer_u_bf'][0], w['peer_vt_bf'][0], row(w['norm_final']), False)

    pp = _norm_matmul(x, row(w['norm_mix'][1]), w['cf_w_pw1'], row(w['cf_b_pw1'][0]))
    x, cf_cs = _cf_block(
        x.reshape(t, nb, d), pp.reshape(t, nb, 2 * d), jnp.transpose(cf_conv0[0], (1, 0, 2)),
        w['cf_dw_w'][0], row(w['cf_dw_b'][0]), row(w['cf_ln_g'][0]), row(w['cf_ln_b'][0]),
        w['cf_w_pw2'], row(w['cf_b_pw2'][0]))
    y = _peer(x.reshape(n, d), row(w['norm_ffn'][1]), w['peer_wqt'][1], w['peer_keys'][1],
              w['peer_u_bf'][1], w['peer_vt_bf'][1], row(w['norm_final']), True)

    y = jnp.transpose(y.reshape(t, nb, d), (1, 0, 2))
    return (y, h_last[None], jnp.transpose(rg_cs, (1, 0, 2))[None],
            jnp.transpose(cf_cs, (1, 0, 2))[None])


def kernel(x_prompt, x_sample, state_rglru_h, state_rglru_conv, state_conformer_conv, norm_mix, norm_ffn, norm_final, rg_w_in, rg_conv_w, rg_conv_b, rg_w_a, rg_b_a, rg_w_i, rg_b_i, rg_lambda, rg_w_out, cf_w_pw1, cf_b_pw1, cf_dw_w, cf_dw_b, cf_ln_g, cf_ln_b, cf_w_pw2, cf_b_pw2, peer_w_q, peer_sub_keys, peer_u, peer_v):
    assert norm_mix.shape[0] == 2 and rg_w_in.shape[0] == 1 and cf_w_pw1.shape[0] == 1
    params = dict(norm_mix=norm_mix, norm_ffn=norm_ffn, norm_final=norm_final,
                  rg_w_in=rg_w_in, rg_conv_w=rg_conv_w, rg_conv_b=rg_conv_b,
                  rg_w_a=rg_w_a, rg_b_a=rg_b_a, rg_w_i=rg_w_i, rg_b_i=rg_b_i,
                  rg_lambda=rg_lambda, rg_w_out=rg_w_out,
                  cf_w_pw1=cf_w_pw1, cf_b_pw1=cf_b_pw1, cf_dw_w=cf_dw_w, cf_dw_b=cf_dw_b,
                  cf_ln_g=cf_ln_g, cf_ln_b=cf_ln_b, cf_w_pw2=cf_w_pw2, cf_b_pw2=cf_b_pw2,
                  peer_w_q=peer_w_q, peer_sub_keys=peer_sub_keys, peer_u=peer_u, peer_v=peer_v)
    w = _prep_weights(params)
    dt = x_prompt.dtype
    nbp, _, d = x_prompt.shape
    n_a, n_b = state_rglru_h.shape[0], state_conformer_conv.shape[0]
    h0_p = jnp.zeros((n_a, nbp, d), dt)
    rgc0_p = jnp.zeros((n_a, nbp) + state_rglru_conv.shape[2:], dt)
    cfc0_p = jnp.zeros((n_b, nbp) + state_conformer_conv.shape[2:], dt)
    y_p, p_h, p_rgc, p_cfc = _trunk(x_prompt, h0_p, rgc0_p, cfc0_p, w)
    y_s, s_h, s_rgc, s_cfc = _trunk(x_sample, state_rglru_h, state_rglru_conv,
                                    state_conformer_conv, w)
    return (y_p, y_s, p_h, p_rgc, p_cfc, s_h, s_rgc, s_cfc)
```

```python
import functools

import jax
import jax.numpy as jnp
from jax import lax
from jax.experimental import pallas as pl
from jax.experimental.pallas import tpu as pltpu

F32 = jnp.float32
BF16 = jnp.bfloat16

EPS = 1e-6
RG_C = 8.0
TOPK = 16

LANES = 128
SUBLANES = 8
MXU_DIM = 256
VMEM_LIMIT = 56 * 1024 * 1024

ROW_TILE = 512
PEER_TOK_TILE = 512
PEER_EXP_TILE = 1024


def _rms(x, g):
    return x * lax.rsqrt(jnp.mean(x * x, axis=-1, keepdims=True) + EPS) * g


def _cparams(sem):
    return pltpu.CompilerParams(dimension_semantics=sem, vmem_limit_bytes=VMEM_LIMIT)


def _norm_matmul_kernel(x_ref, g_ref, w_ref, b_ref, o_ref):
    xn = _rms(x_ref[...], g_ref[...])
    o_ref[...] = jnp.dot(xn.astype(BF16), w_ref[...], preferred_element_type=F32) + b_ref[...]


def _norm_matmul(x, g, w, b):
    n, d = x.shape
    n_out = w.shape[1]
    tm = min(ROW_TILE, n)
    return pl.pallas_call(
        _norm_matmul_kernel,
        grid=(n // tm,),
        in_specs=[
            pl.BlockSpec((tm, d), lambda i: (i, 0)),
            pl.BlockSpec((1, d), lambda i: (0, 0)),
            pl.BlockSpec((d, n_out), lambda i: (0, 0)),
            pl.BlockSpec((1, n_out), lambda i: (0, 0)),
        ],
        out_specs=pl.BlockSpec((tm, n_out), lambda i: (i, 0)),
        out_shape=jax.ShapeDtypeStruct((n, n_out), F32),
        compiler_params=_cparams(("parallel",)),
        name="norm_matmul",
    )(x, g, w, b)


def _rg_kernel(x_ref, proj_ref, h0_ref, conv0_ref, cw_ref, cb_ref, wa_ref, ba_ref, wi_ref,
               bi_ref, lam_ref, wout_ref, o_ref, hlast_ref, cstate_ref,
               ext_ref, a_ref, u_ref, hcar_ref):
    tc, nb, d = x_ref.shape
    rows = tc * nb
    cw_taps = cw_ref.shape[0]
    hist = cw_taps - 1
    step = pl.program_id(1)

    @pl.when(step == 0)
    def _():
        ext_ref[0:hist] = conv0_ref[...]
        hcar_ref[...] = h0_ref[...]

    @pl.when(step > 0)
    def _():
        ext_ref[0:hist] = ext_ref[tc:tc + hist]

    ext_ref[hist:hist + tc] = proj_ref[:, :, d:]

    xc = cb_ref[...][None]
    for k in range(cw_taps):
        xc = xc + cw_ref[k:k + 1, :][None] * ext_ref[k:k + tc]
    xc2 = xc.reshape(rows, d)
    xcb = xc2.astype(BF16)

    def blockdiag(w_ref):
        n_grp = w_ref.shape[0]
        gw = d // n_grp
        outs = [jnp.dot(xcb[:, g * gw:(g + 1) * gw], w_ref[g], preferred_element_type=F32)
                for g in range(n_grp)]
        return jnp.concatenate(outs, axis=-1)

    r = jax.nn.sigmoid(blockdiag(wa_ref) + ba_ref[...])
    gi = jax.nn.sigmoid(blockdiag(wi_ref) + bi_ref[...])
    neg_lam = -lam_ref[...]
    softplus = jnp.maximum(neg_lam, 0.0) + jnp.log1p(jnp.exp(-jnp.abs(neg_lam)))
    log_a = (-RG_C) * r * softplus
    a = jnp.exp(log_a)
    u = jnp.sqrt(-jnp.tanh(log_a) * (a * a + 1.0)) * (gi * xc2)
    a_ref[...] = a.reshape(tc, nb, d)
    u_ref[...] = u.reshape(tc, nb, d)

    def scan_body(t, h):
        h = a_ref[t] * h + u_ref[t]
        u_ref[t] = h
        return h

    h_fin = lax.fori_loop(0, tc, scan_body, hcar_ref[...], unroll=min(tc, 8))
    hcar_ref[...] = h_fin

    gate = jax.nn.gelu(proj_ref[:, :, :d].reshape(rows, d))
    gated = (u_ref[...].reshape(rows, d) * gate).astype(BF16)
    y = jnp.dot(gated, wout_ref[...], preferred_element_type=F32)
    o_ref[...] = x_ref[...] + y.reshape(tc, nb, d)

    @pl.when(step == pl.num_programs(1) - 1)
    def _():
        hlast_ref[...] = h_fin
        cstate_ref[...] = ext_ref[tc:tc + hist]


def _seq_tiles(t, nb, max_bb):
    bb = min(nb, max_bb, max(SUBLANES, ROW_TILE // t))
    tc = min(t, ROW_TILE // bb)
    assert nb % bb == 0 and t % tc == 0 and bb % SUBLANES == 0
    return bb, tc


def _rg_block(x_tm, proj_tm, h0, conv0, cw, cb, wa_bd, ba, wi_bd, bi, lam, wout):
    t, nb, d = x_tm.shape
    bb, tc = _seq_tiles(t, nb, ROW_TILE)
    hist = cw.shape[0] - 1
    assert t == tc or tc >= hist
    const2 = lambda b, i: (0, 0)
    const3 = lambda b, i: (0, 0, 0)
    return pl.pallas_call(
        _rg_kernel,
        grid=(nb // bb, t // tc),
        in_specs=[
            pl.BlockSpec((tc, bb, d), lambda b, i: (i, b, 0)),
            pl.BlockSpec((tc, bb, 2 * d), lambda b, i: (i, b, 0)),
            pl.BlockSpec((bb, d), lambda b, i: (b, 0)),
            pl.BlockSpec((hist, bb, d), lambda b, i: (0, b, 0)),
            pl.BlockSpec(cw.shape, const2),
            pl.BlockSpec((1, d), const2),
            pl.BlockSpec(wa_bd.shape, const3),
            pl.BlockSpec((1, d), const2),
            pl.BlockSpec(wi_bd.shape, const3),
            pl.BlockSpec((1, d), const2),
            pl.BlockSpec((1, d), const2),
            pl.BlockSpec((d, d), const2),
        ],
        out_specs=[
            pl.BlockSpec((tc, bb, d), lambda b, i: (i, b, 0)),
            pl.BlockSpec((bb, d), lambda b, i: (b, 0)),
            pl.BlockSpec((hist, bb, d), lambda b, i: (0, b, 0)),
        ],
        out_shape=[
            jax.ShapeDtypeStruct((t, nb, d), F32),
            jax.ShapeDtypeStruct((nb, d), F32),
            jax.ShapeDtypeStruct((hist, nb, d), F32),
        ],
        scratch_shapes=[
            pltpu.VMEM((tc + hist, bb, d), F32),
            pltpu.VMEM((tc, bb, d), F32),
            pltpu.VMEM((tc, bb, d), F32),
            pltpu.VMEM((bb, d), F32),
        ],
        compiler_params=_cparams(("parallel", "arbitrary")),
        name="rg_block",
    )(x_tm, proj_tm, h0, conv0, cw, cb, wa_bd, ba, wi_bd, bi, lam, wout)


CONV_T_SUB = 4
CF_MAX_BATCH_ROWS = 32


def _cf_kernel(x_ref, p_ref, conv0_ref, dw_ref, dwb_ref, lng_ref, lnb_ref, w2_ref, b2_ref,
               o_ref, cstate_ref, ext_ref, c_ref, wb_ref):
    tc, nb, d = x_ref.shape
    rows = tc * nb
    taps = dw_ref.shape[0]
    hist = taps - 1
    step = pl.program_id(1)

    @pl.when(step == 0)
    def _():
        ext_ref[0:hist] = conv0_ref[...]
        for k in range(taps):
            wb_ref[k] = jnp.broadcast_to(dw_ref[k:k + 1, :], (SUBLANES, d))

    @pl.when(step > 0)
    def _():
        ext_ref[0:hist] = ext_ref[tc:tc + hist]

    ext_ref[hist:hist + tc] = p_ref[:, :, :d] * jax.nn.sigmoid(p_ref[:, :, d:])

    n_b = nb // SUBLANES
    bias = jnp.broadcast_to(dwb_ref[...], (SUBLANES, d))[None]

    def conv_body(idx, carry):
        t0 = (idx // n_b) * CONV_T_SUB
        b0 = pl.multiple_of((idx % n_b) * SUBLANES, SUBLANES)
        acc = jnp.broadcast_to(bias, (CONV_T_SUB, SUBLANES, d))
        for k in range(taps):
            acc = acc + wb_ref[k][None] * ext_ref[pl.ds(t0 + k, CONV_T_SUB), pl.ds(b0, SUBLANES), :]
        c_ref[pl.ds(t0, CONV_T_SUB), pl.ds(b0, SUBLANES), :] = acc
        return carry

    lax.fori_loop(0, (tc // CONV_T_SUB) * n_b, conv_body, 0)

    c = c_ref[...].reshape(rows, d)
    mu = jnp.mean(c, axis=-1, keepdims=True)
    cc = c - mu
    var = jnp.mean(cc * cc, axis=-1, keepdims=True)
    y = cc * lax.rsqrt(var + EPS) * lng_ref[...] + lnb_ref[...]
    y = jax.nn.silu(y).astype(BF16)
    out = jnp.dot(y, w2_ref[...], preferred_element_type=F32) + b2_ref[...]
    o_ref[...] = x_ref[...] + out.reshape(tc, nb, d)

    @pl.when(step == pl.num_programs(1) - 1)
    def _():
        cstate_ref[...] = ext_ref[tc:tc + hist]


def _cf_block(x_tm, p_tm, conv0, dw, dwb, lng, lnb, w2, b2):
    t, nb, d = x_tm.shape
    bb, tc = _seq_tiles(t, nb, CF_MAX_BATCH_ROWS)
    taps = dw.shape[0]
    hist = taps - 1
    assert tc % CONV_T_SUB == 0
    assert t == tc or tc >= hist
    const2 = lambda b, i: (0, 0)
    return pl.pallas_call(
        _cf_kernel,
        grid=(nb // bb, t // tc),
        in_specs=[
            pl.BlockSpec((tc, bb, d), lambda b, i: (i, b, 0)),
            pl.BlockSpec((tc, bb, 2 * d), lambda b, i: (i, b, 0)),
            pl.BlockSpec((hist, bb, d), lambda b, i: (0, b, 0)),
            pl.BlockSpec(dw.shape, const2),
            pl.BlockSpec((1, d), const2),
            pl.BlockSpec((1, d), const2),
            pl.BlockSpec((1, d), const2),
            pl.BlockSpec((d, d), const2),
            pl.BlockSpec((1, d), const2),
        ],
        out_specs=[
            pl.BlockSpec((tc, bb, d), lambda b, i: (i, b, 0)),
            pl.BlockSpec((hist, bb, d), lambda b, i: (0, b, 0)),
        ],
        out_shape=[
            jax.ShapeDtypeStruct((t, nb, d), F32),
            jax.ShapeDtypeStruct((hist, nb, d), F32),
        ],
        scratch_shapes=[
            pltpu.VMEM((tc + hist, bb, d), F32),
            pltpu.VMEM((tc, bb, d), F32),
            pltpu.VMEM((taps, SUBLANES, d), F32),
        ],
        compiler_params=_cparams(("parallel", "arbitrary")),
        name="cf_block",
    )(x_tm, p_tm, conv0, dw, dwb, lng, lnb, w2, b2)


U32 = jnp.uint32
BF16_ROWS = 16


def _pack_rows(v_bf16):
    return pltpu.bitcast(v_bf16, U32)


def _unpack_rows(words):
    return pltpu.bitcast(words, BF16)


def _dup_bf16_words(v):
    hi = pltpu.bitcast(v.astype(BF16).astype(F32), U32)
    return hi | (hi >> 16)


def _pack_rows_xla(w):
    r, c = w.shape
    bits = lax.bitcast_convert_type(w.astype(BF16), jnp.uint16).astype(U32)
    pairs = bits.reshape(r // 2, 2 * c)
    return pairs[:, :c] | (pairs[:, c:] << 16)


def _gelu_tanh(x):
    k1 = 2.0 * 0.7978845608028654
    k2 = k1 * 0.044715
    neg_z = x * (x * x * (-k2) + (-k1))
    return x / (1.0 + jnp.exp(neg_z))


def _sorted_topk(s, sv_ref, want_rank):
    work = s
    rank = jnp.full(s.shape, float(TOPK), F32) if want_rank else None
    for i in range(TOPK):
        m = jnp.max(work, axis=0, keepdims=True)
        sv_ref[i:i + 1, :] = m
        hit = work >= m
        if want_rank:
            rank = jnp.where(hit, float(i), rank)
        if i + 1 < TOPK:
            work = jnp.where(hit, -jnp.inf, work)
    return rank


def _peer_stats_kernel(x_ref, g_ref, wqt_ref, keys_ref, xnt_ref, rank2_ref, cnt1_ref, e1_ref,
                       e2n_ref, qt_ref, sv1_ref, sv2_ref):
    n_heads = rank2_ref.shape[0]
    xn = _rms(x_ref[...], g_ref[...])
    xnt = xn.T.astype(BF16)
    xnt_ref[...] = _pack_rows(xnt)
    qt_ref[...] = jnp.dot(wqt_ref[...], xnt, preferred_element_type=F32)
    d_half = qt_ref.shape[0] // (2 * n_heads)

    def head_body(h, carry):
        q1 = qt_ref[pl.ds(pl.multiple_of(h * 2 * d_half, d_half), d_half), :]
        q2 = qt_ref[pl.ds(pl.multiple_of(h * 2 * d_half + d_half, d_half), d_half), :]
        s1 = jnp.dot(keys_ref[2 * h], q1, preferred_element_type=F32)
        s2 = jnp.dot(keys_ref[2 * h + 1], q2, preferred_element_type=F32)
        _sorted_topk(s1, sv1_ref, False)
        rank2 = _sorted_topk(s2, sv2_ref, True)
        sv1 = sv1_ref[...]
        sv2 = sv2_ref[...]
        half = TOPK // 2
        cand = [sv1[0:1] + sv2]
        cand += [sv1[i:i + 1] + sv2[0:half] for i in range(1, half)]
        cand += [sv1[half:] + sv2[0:1]]
        work = jnp.concatenate(cand, axis=0)
        tau = None
        for i in range(TOPK):
            tau = jnp.max(work, axis=0, keepdims=True)
            if i + 1 < TOPK:
                work = jnp.where(work >= tau, -jnp.inf, work)
        e2v = jnp.exp(sv2 - sv2[0:1])
        cnt_rows = jnp.zeros(sv1.shape, F32)
        mass = jnp.zeros(sv1.shape, F32)
        prefix = jnp.zeros_like(tau)
        for j in range(TOPK):
            prefix = prefix + e2v[j:j + 1]
            cond = (sv1 + sv2[j:j + 1]) >= tau
            cnt_rows = cnt_rows + jnp.where(cond, 1.0, 0.0)
            mass = jnp.where(cond, prefix, mass)
        z = jnp.sum(jnp.exp(sv1 - sv1[0:1]) * mass, axis=0, keepdims=True)
        cnt1 = jnp.zeros(s1.shape, F32)
        for i in range(TOPK):
            cnt1 = jnp.where(s1 == sv1[i:i + 1], cnt_rows[i:i + 1], cnt1)
        rank2_ref[h] = _pack_rows(rank2.astype(BF16))
        cnt1_ref[h] = _dup_bf16_words(cnt1)
        e1_ref[h] = _dup_bf16_words(jnp.exp(s1 - sv1[0:1]))
        e2n_ref[h] = _pack_rows((jnp.exp(s2 - sv2[0:1]) / z).astype(BF16))
        return carry

    lax.fori_loop(0, n_heads, head_body, 0)


def _peer_stats(x, g, wqt, keys):
    n, d = x.shape
    tt = min(PEER_TOK_TILE, n)
    n_heads = keys.shape[0] // 2
    n_keys = keys.shape[1]
    row_spec = pl.BlockSpec((n_heads, n_keys, tt), lambda i: (0, 0, i))
    packed_spec = pl.BlockSpec((n_heads, n_keys // 2, tt), lambda i: (0, 0, i))
    row_words = jax.ShapeDtypeStruct((n_heads, n_keys, n), U32)
    packed_words = jax.ShapeDtypeStruct((n_heads, n_keys // 2, n), U32)
    return pl.pallas_call(
        _peer_stats_kernel,
        grid=(n // tt,),
        in_specs=[
            pl.BlockSpec((tt, d), lambda i: (i, 0)),
            pl.BlockSpec((1, d), lambda i: (0, 0)),
            pl.BlockSpec(wqt.shape, lambda i: (0, 0)),
            pl.BlockSpec(keys.shape, lambda i: (0, 0, 0)),
        ],
        out_specs=[pl.BlockSpec((d // 2, tt), lambda i: (0, i)), packed_spec, row_spec, row_spec,
                   packed_spec],
        out_shape=[jax.ShapeDtypeStruct((d // 2, n), U32), packed_words, row_words, row_words,
                   packed_words],
        scratch_shapes=[
            pltpu.VMEM((wqt.shape[0], tt), F32),
            pltpu.VMEM((TOPK, tt), F32),
            pltpu.VMEM((TOPK, tt), F32),
        ],
        compiler_params=_cparams(("parallel",)),
        name="peer_stats",
    )(x, g, wqt, keys)


K1_BLOCK = 2


def _peer_dense_kernel(x_ref, xnt_ref, rank2_ref, cnt1_ref, e1_ref, e2n_ref, u_ref, vt_ref,
                       gf_ref, o_ref, acc_ref, a_ref, w_ref, *, final_norm):
    n_heads, n_keys, tt = cnt1_ref.shape[0], 2 * rank2_ref.shape[1], rank2_ref.shape[2]
    et = 2 * u_ref.shape[0]
    n_k1 = et // n_keys
    j = pl.program_id(1)
    n_j = pl.num_programs(1) - 2
    packed = (n_keys // BF16_ROWS, BF16_ROWS, LANES)

    def activation_matmul(slot):
        a_ref[slot] = jnp.dot(_unpack_rows(u_ref[...]), _unpack_rows(xnt_ref[...]),
                              preferred_element_type=F32)

    def value_matmul(slot):
        return jnp.dot(_unpack_rows(vt_ref[...]), w_ref[slot],
                       preferred_element_type=F32)

    def row_tile(ref, h, kk, cols):
        words = jnp.broadcast_to(ref[h, kk:kk + 1, cols], (SUBLANES, LANES))
        return _unpack_rows(words)[None]

    def build_weighted(slot):
        for kb in range(0, n_k1, K1_BLOCK):
            for c in range(tt // LANES):
                cols = slice(c * LANES, (c + 1) * LANES)
                g = [None] * K1_BLOCK
                for h in range(n_heads):
                    r2 = _unpack_rows(rank2_ref[h, :, cols]).reshape(packed)
                    e2 = _unpack_rows(e2n_ref[h, :, cols]).reshape(packed)
                    for q in range(K1_BLOCK):
                        kk = kb + q
                        sel = jnp.where(r2 < row_tile(cnt1_ref, h, kk, cols), e2,
                                        jnp.zeros_like(e2))
                        term = sel * row_tile(e1_ref, h, kk, cols)
                        g[q] = term if h == 0 else g[q] + term
                for q in range(K1_BLOCK):
                    rows = slice((kb + q) * n_keys, (kb + q + 1) * n_keys)
                    act = _gelu_tanh(a_ref[slot, rows, cols].astype(BF16))
                    w_ref[slot, rows, cols] = g[q].reshape(n_keys, LANES) * act

    @pl.when(j == 0)
    def _():
        acc_ref[...] = jnp.zeros_like(acc_ref)
        w_ref[...] = jnp.zeros_like(w_ref)
        activation_matmul(0)

    @pl.when((j >= 1) & (j <= n_j))
    def _():
        cur = j % 2
        activation_matmul(cur)
        build_weighted(1 - cur)
        acc_ref[...] += value_matmul(cur)

    @pl.when(j == n_j + 1)
    def _():
        y = x_ref[...] + (acc_ref[...] + value_matmul(j % 2)).T
        if final_norm:
            y = _rms(y, gf_ref[...])
        o_ref[...] = y


def _peer_dense(x, xnt, rank2, cnt1, e1, e2n, u_pk, vt_bf, g_final, final_norm):
    n, d = x.shape
    tt = min(PEER_TOK_TILE, n)
    n_heads, n_keys, _ = cnt1.shape
    et = PEER_EXP_TILE
    n_j = 2 * u_pk.shape[0] // et
    n_k1 = et // n_keys
    last = n_j - 1
    packed_spec = pl.BlockSpec((n_heads, n_keys // 2, tt), lambda i, j: (0, 0, i))
    k1_spec = pl.BlockSpec((n_heads, n_k1, tt),
                           lambda i, j: (0, jnp.clip(j - 1, 0, last), i))
    return pl.pallas_call(
        functools.partial(_peer_dense_kernel, final_norm=final_norm),
        grid=(n // tt, n_j + 2),
        in_specs=[
            pl.BlockSpec((tt, d), lambda i, j: (i, 0)),
            pl.BlockSpec((d // 2, tt), lambda i, j: (0, i)),
            packed_spec, k1_spec, k1_spec, packed_spec,
            pl.BlockSpec((et // 2, d), lambda i, j: (jnp.minimum(j, last), 0)),
            pl.BlockSpec((d // 2, et), lambda i, j: (0, jnp.clip(j - 2, 0, last))),
            pl.BlockSpec((1, d), lambda i, j: (0, 0)),
        ],
        out_specs=pl.BlockSpec((tt, d), lambda i, j: (i, 0)),
        out_shape=jax.ShapeDtypeStruct((n, d), F32),
        scratch_shapes=[
            pltpu.VMEM((d, tt), F32),
            pltpu.VMEM((2, et, tt), F32),
            pltpu.VMEM((2, et, tt), BF16),
        ],
        compiler_params=_cparams(("parallel", "arbitrary")),
        name="peer_dense",
    )(x, xnt, rank2, cnt1, e1, e2n, u_pk, vt_bf, g_final)


def _peer(x, g, wqt, keys, u_pk, vt_bf, g_final, final_norm):
    xnt, rank2, cnt1, e1, e2n = _peer_stats(x, g, wqt, keys)
    return _peer_dense(x, xnt, rank2, cnt1, e1, e2n, u_pk, vt_bf, g_final, final_norm)


def _block_diag_groups(w):
    n_blocks, bs, _ = w.shape
    per = MXU_DIM // bs
    w4 = w.reshape(n_blocks // per, per, bs, bs)
    eye = jnp.eye(per, dtype=w.dtype)
    bd = jnp.einsum('gaij,ab->gaibj', w4, eye)
    return bd.reshape(n_blocks // per, MXU_DIM, MXU_DIM).astype(BF16)


def _prep_weights(p):
    row = lambda v: v.reshape(1, -1)
    depth = p['norm_mix'].shape[0]
    w = dict(p)
    w['rg_w_in'] = p['rg_w_in'][0].astype(BF16)
    w['rg_wa_bd'] = _block_diag_groups(p['rg_w_a'][0])
    w['rg_wi_bd'] = _block_diag_groups(p['rg_w_i'][0])
    w['rg_w_out'] = p['rg_w_out'][0].astype(BF16)
    w['cf_w_pw1'] = p['cf_w_pw1'][0].astype(BF16)
    w['cf_w_pw2'] = p['cf_w_pw2'][0].astype(BF16)
    w['peer_wqt'] = [p['peer_w_q'][i].T.astype(BF16) for i in range(depth)]
    w['peer_keys'] = [p['peer_sub_keys'][i].reshape((-1,) + p['peer_sub_keys'].shape[-2:])
                      for i in range(depth)]
    w['peer_u_pk'] = [_pack_rows_xla(p['peer_u'][i]) for i in range(depth)]
    w['peer_vt_bf'] = [_pack_rows_xla(p['peer_v'][i].T) for i in range(depth)]
    w['row'] = row
    return w


def _trunk(x_btd, rg_h0, rg_conv0, cf_conv0, w):
    nb, t, d = x_btd.shape
    row = w['row']
    x = jnp.transpose(x_btd, (1, 0, 2))
    n = t * nb
    zeros2d = jnp.zeros((1, 2 * d), F32)

    proj = _norm_matmul(x.reshape(n, d), row(w['norm_mix'][0]), w['rg_w_in'], zeros2d)
    x, h_last, rg_cs = _rg_block(
        x, proj.reshape(t, nb, 2 * d), rg_h0[0], jnp.transpose(rg_conv0[0], (1, 0, 2)),
        w['rg_conv_w'][0], row(w['rg_conv_b'][0]), w['rg_wa_bd'], row(w['rg_b_a'][0]),
        w['rg_wi_bd'], row(w['rg_b_i'][0]), row(w['rg_lambda'][0]), w['rg_w_out'])
    x = _peer(x.reshape(n, d), row(w['norm_ffn'][0]), w['peer_wqt'][0], w['peer_keys'][0],
              w['peer_u_pk'][0], w['peer_vt_bf'][0], row(w['norm_final']), False)

    pp = _norm_matmul(x, row(w['norm_mix'][1]), w['cf_w_pw1'], row(w['cf_b_pw1'][0]))
    x, cf_cs = _cf_block(
        x.reshape(t, nb, d), pp.reshape(t, nb, 2 * d), jnp.transpose(cf_conv0[0], (1, 0, 2)),
        w['cf_dw_w'][0], row(w['cf_dw_b'][0]), row(w['cf_ln_g'][0]), row(w['cf_ln_b'][0]),
        w['cf_w_pw2'], row(w['cf_b_pw2'][0]))
    y = _peer(x.reshape(n, d), row(w['norm_ffn'][1]), w['peer_wqt'][1], w['peer_keys'][1],
              w['peer_u_pk'][1], w['peer_vt_bf'][1], row(w['norm_final']), True)

    y = jnp.transpose(y.reshape(t, nb, d), (1, 0, 2))
    return (y, h_last[None], jnp.transpose(rg_cs, (1, 0, 2))[None],
            jnp.transpose(cf_cs, (1, 0, 2))[None])


def kernel(x_prompt, x_sample, state_rglru_h, state_rglru_conv, state_conformer_conv, norm_mix, norm_ffn, norm_final, rg_w_in, rg_conv_w, rg_conv_b, rg_w_a, rg_b_a, rg_w_i, rg_b_i, rg_lambda, rg_w_out, cf_w_pw1, cf_b_pw1, cf_dw_w, cf_dw_b, cf_ln_g, cf_ln_b, cf_w_pw2, cf_b_pw2, peer_w_q, peer_sub_keys, peer_u, peer_v):
    assert norm_mix.shape[0] == 2 and rg_w_in.shape[0] == 1 and cf_w_pw1.shape[0] == 1
    params = dict(norm_mix=norm_mix, norm_ffn=norm_ffn, norm_final=norm_final,
                  rg_w_in=rg_w_in, rg_conv_w=rg_conv_w, rg_conv_b=rg_conv_b,
                  rg_w_a=rg_w_a, rg_b_a=rg_b_a, rg_w_i=rg_w_i, rg_b_i=rg_b_i,
                  rg_lambda=rg_lambda, rg_w_out=rg_w_out,
                  cf_w_pw1=cf_w_pw1, cf_b_pw1=cf_b_pw1, cf_dw_w=cf_dw_w, cf_dw_b=cf_dw_b,
                  cf_ln_g=cf_ln_g, cf_ln_b=cf_ln_b, cf_w_pw2=cf_w_pw2, cf_b_pw2=cf_b_pw2,
                  peer_w_q=peer_w_q, peer_sub_keys=peer_sub_keys, peer_u=peer_u, peer_v=peer_v)
    w = _prep_weights(params)
    dt = x_prompt.dtype
    nbp, _, d = x_prompt.shape
    n_a, n_b = state_rglru_h.shape[0], state_conformer_conv.shape[0]
    h0_p = jnp.zeros((n_a, nbp, d), dt)
    rgc0_p = jnp.zeros((n_a, nbp) + state_rglru_conv.shape[2:], dt)
    cfc0_p = jnp.zeros((n_b, nbp) + state_conformer_conv.shape[2:], dt)
    y_p, p_h, p_rgc, p_cfc = _trunk(x_prompt, h0_p, rgc0_p, cfc0_p, w)
    y_s, s_h, s_rgc, s_cfc = _trunk(x_sample, state_rglru_h, state_rglru_conv,
                                    state_conformer_conv, w)
    return (y_p, y_s, p_h, p_rgc, p_cfc, s_h, s_rgc, s_cfc)
```

```python
import functools

import jax
import jax.numpy as jnp
import numpy as np
from jax import lax
from jax.experimental import pallas as pl
from jax.experimental.pallas import tpu as pltpu

F32 = jnp.float32
BF16 = jnp.bfloat16

EPS = 1e-6
RG_C = 8.0
TOPK = 16

LANES = 128
SUBLANES = 8
MXU_DIM = 256
VMEM_LIMIT = 56 * 1024 * 1024

ROW_TILE = 512
PEER_TOK_TILE = 512
PEER_EXP_TILE = 1024


def _rms(x, g):
    return x * lax.rsqrt(jnp.mean(x * x, axis=-1, keepdims=True) + EPS) * g


def _cparams(sem):
    return pltpu.CompilerParams(dimension_semantics=sem, vmem_limit_bytes=VMEM_LIMIT)


def _norm_matmul_kernel(x_ref, g_ref, w_ref, b_ref, o_ref):
    xn = _rms(x_ref[...], g_ref[...])
    o_ref[...] = jnp.dot(xn.astype(BF16), w_ref[...], preferred_element_type=F32) + b_ref[...]


def _norm_matmul(x, g, w, b):
    n, d = x.shape
    n_out = w.shape[1]
    tm = min(ROW_TILE, n)
    return pl.pallas_call(
        _norm_matmul_kernel,
        grid=(n // tm,),
        in_specs=[
            pl.BlockSpec((tm, d), lambda i: (i, 0)),
            pl.BlockSpec((1, d), lambda i: (0, 0)),
            pl.BlockSpec((d, n_out), lambda i: (0, 0)),
            pl.BlockSpec((1, n_out), lambda i: (0, 0)),
        ],
        out_specs=pl.BlockSpec((tm, n_out), lambda i: (i, 0)),
        out_shape=jax.ShapeDtypeStruct((n, n_out), F32),
        compiler_params=_cparams(("parallel",)),
        name="norm_matmul",
    )(x, g, w, b)


def _rg_kernel(x_ref, proj_ref, h0_ref, conv0_ref, cw_ref, cb_ref, wa_ref, ba_ref, wi_ref,
               bi_ref, lam_ref, wout_ref, o_ref, hlast_ref, cstate_ref,
               ext_ref, a_ref, u_ref, hcar_ref):
    tc, nb, d = x_ref.shape
    rows = tc * nb
    cw_taps = cw_ref.shape[0]
    hist = cw_taps - 1
    step = pl.program_id(1)

    @pl.when(step == 0)
    def _():
        ext_ref[0:hist] = conv0_ref[...]
        hcar_ref[...] = h0_ref[...]

    @pl.when(step > 0)
    def _():
        ext_ref[0:hist] = ext_ref[tc:tc + hist]

    ext_ref[hist:hist + tc] = proj_ref[:, :, d:]

    xc = cb_ref[...][None]
    for k in range(cw_taps):
        xc = xc + cw_ref[k:k + 1, :][None] * ext_ref[k:k + tc]
    xc2 = xc.reshape(rows, d)
    xcb = xc2.astype(BF16)

    def blockdiag(w_ref):
        n_grp = w_ref.shape[0]
        gw = d // n_grp
        outs = [jnp.dot(xcb[:, g * gw:(g + 1) * gw], w_ref[g], preferred_element_type=F32)
                for g in range(n_grp)]
        return jnp.concatenate(outs, axis=-1)

    r = jax.nn.sigmoid(blockdiag(wa_ref) + ba_ref[...])
    gi = jax.nn.sigmoid(blockdiag(wi_ref) + bi_ref[...])
    neg_lam = -lam_ref[...]
    softplus = jnp.maximum(neg_lam, 0.0) + jnp.log1p(jnp.exp(-jnp.abs(neg_lam)))
    log_a = (-RG_C) * r * softplus
    a = jnp.exp(log_a)
    u = jnp.sqrt(-jnp.tanh(log_a) * (a * a + 1.0)) * (gi * xc2)
    a_ref[...] = a.reshape(tc, nb, d)
    u_ref[...] = u.reshape(tc, nb, d)

    def scan_body(t, h):
        h = a_ref[t] * h + u_ref[t]
        u_ref[t] = h
        return h

    h_fin = lax.fori_loop(0, tc, scan_body, hcar_ref[...], unroll=min(tc, 8))
    hcar_ref[...] = h_fin

    gate = jax.nn.gelu(proj_ref[:, :, :d].reshape(rows, d))
    gated = (u_ref[...].reshape(rows, d) * gate).astype(BF16)
    y = jnp.dot(gated, wout_ref[...], preferred_element_type=F32)
    o_ref[...] = x_ref[...] + y.reshape(tc, nb, d)

    @pl.when(step == pl.num_programs(1) - 1)
    def _():
        hlast_ref[...] = h_fin
        cstate_ref[...] = ext_ref[tc:tc + hist]


def _seq_tiles(t, nb, max_bb):
    bb = min(nb, max_bb, max(SUBLANES, ROW_TILE // t))
    tc = min(t, ROW_TILE // bb)
    assert nb % bb == 0 and t % tc == 0 and bb % SUBLANES == 0
    return bb, tc


def _rg_block(x_tm, proj_tm, h0, conv0, cw, cb, wa_bd, ba, wi_bd, bi, lam, wout):
    t, nb, d = x_tm.shape
    bb, tc = _seq_tiles(t, nb, ROW_TILE)
    hist = cw.shape[0] - 1
    assert t == tc or tc >= hist
    const2 = lambda b, i: (0, 0)
    const3 = lambda b, i: (0, 0, 0)
    return pl.pallas_call(
        _rg_kernel,
        grid=(nb // bb, t // tc),
        in_specs=[
            pl.BlockSpec((tc, bb, d), lambda b, i: (i, b, 0)),
            pl.BlockSpec((tc, bb, 2 * d), lambda b, i: (i, b, 0)),
            pl.BlockSpec((bb, d), lambda b, i: (b, 0)),
            pl.BlockSpec((hist, bb, d), lambda b, i: (0, b, 0)),
            pl.BlockSpec(cw.shape, const2),
            pl.BlockSpec((1, d), const2),
            pl.BlockSpec(wa_bd.shape, const3),
            pl.BlockSpec((1, d), const2),
            pl.BlockSpec(wi_bd.shape, const3),
            pl.BlockSpec((1, d), const2),
            pl.BlockSpec((1, d), const2),
            pl.BlockSpec((d, d), const2),
        ],
        out_specs=[
            pl.BlockSpec((tc, bb, d), lambda b, i: (i, b, 0)),
            pl.BlockSpec((bb, d), lambda b, i: (b, 0)),
            pl.BlockSpec((hist, bb, d), lambda b, i: (0, b, 0)),
        ],
        out_shape=[
            jax.ShapeDtypeStruct((t, nb, d), F32),
            jax.ShapeDtypeStruct((nb, d), F32),
            jax.ShapeDtypeStruct((hist, nb, d), F32),
        ],
        scratch_shapes=[
            pltpu.VMEM((tc + hist, bb, d), F32),
            pltpu.VMEM((tc, bb, d), F32),
            pltpu.VMEM((tc, bb, d), F32),
            pltpu.VMEM((bb, d), F32),
        ],
        compiler_params=_cparams(("parallel", "arbitrary")),
        name="rg_block",
    )(x_tm, proj_tm, h0, conv0, cw, cb, wa_bd, ba, wi_bd, bi, lam, wout)


CONV_T_SUB = 4
CF_MAX_BATCH_ROWS = 32


def _cf_kernel(x_ref, p_ref, conv0_ref, dw_ref, dwb_ref, lng_ref, lnb_ref, w2_ref, b2_ref,
               o_ref, cstate_ref, ext_ref, c_ref, wb_ref):
    tc, nb, d = x_ref.shape
    rows = tc * nb
    taps = dw_ref.shape[0]
    hist = taps - 1
    step = pl.program_id(1)

    @pl.when(step == 0)
    def _():
        ext_ref[0:hist] = conv0_ref[...]
        for k in range(taps):
            wb_ref[k] = jnp.broadcast_to(dw_ref[k:k + 1, :], (SUBLANES, d))

    @pl.when(step > 0)
    def _():
        ext_ref[0:hist] = ext_ref[tc:tc + hist]

    ext_ref[hist:hist + tc] = p_ref[:, :, :d] * jax.nn.sigmoid(p_ref[:, :, d:])

    n_b = nb // SUBLANES
    bias = jnp.broadcast_to(dwb_ref[...], (SUBLANES, d))[None]

    def conv_body(idx, carry):
        t0 = (idx // n_b) * CONV_T_SUB
        b0 = pl.multiple_of((idx % n_b) * SUBLANES, SUBLANES)
        acc = jnp.broadcast_to(bias, (CONV_T_SUB, SUBLANES, d))
        for k in range(taps):
            acc = acc + wb_ref[k][None] * ext_ref[pl.ds(t0 + k, CONV_T_SUB), pl.ds(b0, SUBLANES), :]
        c_ref[pl.ds(t0, CONV_T_SUB), pl.ds(b0, SUBLANES), :] = acc
        return carry

    lax.fori_loop(0, (tc // CONV_T_SUB) * n_b, conv_body, 0)

    c = c_ref[...].reshape(rows, d)
    mu = jnp.mean(c, axis=-1, keepdims=True)
    cc = c - mu
    var = jnp.mean(cc * cc, axis=-1, keepdims=True)
    y = cc * lax.rsqrt(var + EPS) * lng_ref[...] + lnb_ref[...]
    y = jax.nn.silu(y).astype(BF16)
    out = jnp.dot(y, w2_ref[...], preferred_element_type=F32) + b2_ref[...]
    o_ref[...] = x_ref[...] + out.reshape(tc, nb, d)

    @pl.when(step == pl.num_programs(1) - 1)
    def _():
        cstate_ref[...] = ext_ref[tc:tc + hist]


def _cf_block(x_tm, p_tm, conv0, dw, dwb, lng, lnb, w2, b2):
    t, nb, d = x_tm.shape
    bb, tc = _seq_tiles(t, nb, CF_MAX_BATCH_ROWS)
    taps = dw.shape[0]
    hist = taps - 1
    assert tc % CONV_T_SUB == 0
    assert t == tc or tc >= hist
    const2 = lambda b, i: (0, 0)
    return pl.pallas_call(
        _cf_kernel,
        grid=(nb // bb, t // tc),
        in_specs=[
            pl.BlockSpec((tc, bb, d), lambda b, i: (i, b, 0)),
            pl.BlockSpec((tc, bb, 2 * d), lambda b, i: (i, b, 0)),
            pl.BlockSpec((hist, bb, d), lambda b, i: (0, b, 0)),
            pl.BlockSpec(dw.shape, const2),
            pl.BlockSpec((1, d), const2),
            pl.BlockSpec((1, d), const2),
            pl.BlockSpec((1, d), const2),
            pl.BlockSpec((d, d), const2),
            pl.BlockSpec((1, d), const2),
        ],
        out_specs=[
            pl.BlockSpec((tc, bb, d), lambda b, i: (i, b, 0)),
            pl.BlockSpec((hist, bb, d), lambda b, i: (0, b, 0)),
        ],
        out_shape=[
            jax.ShapeDtypeStruct((t, nb, d), F32),
            jax.ShapeDtypeStruct((hist, nb, d), F32),
        ],
        scratch_shapes=[
            pltpu.VMEM((tc + hist, bb, d), F32),
            pltpu.VMEM((tc, bb, d), F32),
            pltpu.VMEM((taps, SUBLANES, d), F32),
        ],
        compiler_params=_cparams(("parallel", "arbitrary")),
        name="cf_block",
    )(x_tm, p_tm, conv0, dw, dwb, lng, lnb, w2, b2)


U32 = jnp.uint32
BF16_ROWS = 16


def _pack_rows(v_bf16):
    return pltpu.bitcast(v_bf16, U32)


def _unpack_rows(words):
    return pltpu.bitcast(words, BF16)


def _dup_bf16_words(v):
    hi = pltpu.bitcast(v.astype(BF16).astype(F32), U32)
    return hi | (hi >> 16)


def _pack_row_groups_xla(w):
    r, c = w.shape
    groups = w.reshape(r // BF16_ROWS, 2, SUBLANES, c)

    def high_half_bits(x):
        return lax.bitcast_convert_type(x.astype(BF16).astype(F32), U32)

    words = (high_half_bits(groups[:, 0]) >> 16) | high_half_bits(groups[:, 1])
    return words.reshape(r // 2, c)


def _paired_order(n):
    pos = np.arange(n)
    grp, q = pos // BF16_ROWS, pos % BF16_ROWS
    return grp * BF16_ROWS + (q % 2) * SUBLANES + q // 2


def _gelu_tanh(x):
    k1 = 2.0 * 0.7978845608028654
    k2 = k1 * 0.044715
    neg_z = x * (x * x * (-k2) + (-k1))
    return x / (1.0 + jnp.exp(neg_z))


def _sorted_topk(s, sv_ref, want_rank):
    work = s
    rank = jnp.full(s.shape, float(TOPK), F32) if want_rank else None
    for i in range(TOPK):
        m = jnp.max(work, axis=0, keepdims=True)
        sv_ref[i:i + 1, :] = m
        hit = work >= m
        if want_rank:
            rank = jnp.where(hit, float(i), rank)
        if i + 1 < TOPK:
            work = jnp.where(hit, -jnp.inf, work)
    return rank


def _peer_stats_kernel(x_ref, g_ref, wqt_ref, keys_ref, xnt_ref, rank2_ref, cnt1_ref, e1_ref,
                       e2n_ref, qt_ref, s_ref, sv1_ref, sv2_ref):
    n_heads = rank2_ref.shape[0]
    xn = _rms(x_ref[...], g_ref[...])
    xnt = xn.T.astype(BF16)
    xnt_ref[...] = _pack_rows(xnt)
    qt_ref[...] = jnp.dot(wqt_ref[...], xnt, preferred_element_type=F32)
    d_half = qt_ref.shape[0] // (2 * n_heads)

    def head_body(h, carry):
        q1 = qt_ref[pl.ds(pl.multiple_of(h * 2 * d_half, d_half), d_half), :]
        q2 = qt_ref[pl.ds(pl.multiple_of(h * 2 * d_half + d_half, d_half), d_half), :]
        s_ref[0] = jnp.dot(keys_ref[2 * h], q1, preferred_element_type=F32)
        s_ref[1] = jnp.dot(keys_ref[2 * h + 1], q2, preferred_element_type=F32)
        for c in range(s_ref.shape[2] // LANES):
            lane_tile_stats(h, slice(c * LANES, (c + 1) * LANES))
        return carry

    def lane_tile_stats(h, cols):
        s1 = s_ref[0, :, cols]
        s2 = s_ref[1, :, cols]
        _sorted_topk(s1, sv1_ref, False)
        rank2 = _sorted_topk(s2, sv2_ref, True)
        sv1 = sv1_ref[...]
        sv2 = sv2_ref[...]
        half = TOPK // 2
        cand = [sv1[0:1] + sv2]
        cand += [sv1[i:i + 1] + sv2[0:half] for i in range(1, half)]
        cand += [sv1[half:] + sv2[0:1]]
        work = jnp.concatenate(cand, axis=0)
        tau = None
        for i in range(TOPK):
            tau = jnp.max(work, axis=0, keepdims=True)
            if i + 1 < TOPK:
                work = jnp.where(work >= tau, -jnp.inf, work)
        e2v = jnp.exp(sv2 - sv2[0:1])
        cnt_rows = jnp.zeros(sv1.shape, F32)
        mass = jnp.zeros(sv1.shape, F32)
        prefix = jnp.zeros_like(tau)
        for j in range(TOPK):
            prefix = prefix + e2v[j:j + 1]
            cond = (sv1 + sv2[j:j + 1]) >= tau
            cnt_rows = cnt_rows + jnp.where(cond, 1.0, 0.0)
            mass = jnp.where(cond, prefix, mass)
        z = jnp.sum(jnp.exp(sv1 - sv1[0:1]) * mass, axis=0, keepdims=True)
        cnt1 = jnp.zeros(s1.shape, F32)
        for i in range(TOPK):
            cnt1 = jnp.where(s1 == sv1[i:i + 1], cnt_rows[i:i + 1], cnt1)
        rank2_ref[h, :, cols] = _pack_rows(rank2.astype(BF16))
        cnt1_ref[h, :, cols] = _dup_bf16_words(cnt1)
        e1_ref[h, :, cols] = _dup_bf16_words(jnp.exp(s1 - sv1[0:1]))
        e2n_ref[h, :, cols] = _pack_rows((jnp.exp(s2 - sv2[0:1]) / z).astype(BF16))

    lax.fori_loop(0, n_heads, head_body, 0)


def _peer_stats(x, g, wqt, keys):
    n, d = x.shape
    tt = min(PEER_TOK_TILE, n)
    n_heads = keys.shape[0] // 2
    n_keys = keys.shape[1]
    row_spec = pl.BlockSpec((n_heads, n_keys, tt), lambda i: (0, 0, i))
    packed_spec = pl.BlockSpec((n_heads, n_keys // 2, tt), lambda i: (0, 0, i))
    row_words = jax.ShapeDtypeStruct((n_heads, n_keys, n), U32)
    packed_words = jax.ShapeDtypeStruct((n_heads, n_keys // 2, n), U32)
    return pl.pallas_call(
        _peer_stats_kernel,
        grid=(n // tt,),
        in_specs=[
            pl.BlockSpec((tt, d), lambda i: (i, 0)),
            pl.BlockSpec((1, d), lambda i: (0, 0)),
            pl.BlockSpec(wqt.shape, lambda i: (0, 0)),
            pl.BlockSpec(keys.shape, lambda i: (0, 0, 0)),
        ],
        out_specs=[pl.BlockSpec((d // 2, tt), lambda i: (0, i)), packed_spec, row_spec, row_spec,
                   packed_spec],
        out_shape=[jax.ShapeDtypeStruct((d // 2, n), U32), packed_words, row_words, row_words,
                   packed_words],
        scratch_shapes=[
            pltpu.VMEM((wqt.shape[0], tt), F32),
            pltpu.VMEM((2, n_keys, tt), F32),
            pltpu.VMEM((TOPK, LANES), F32),
            pltpu.VMEM((TOPK, LANES), F32),
        ],
        compiler_params=_cparams(("parallel",)),
        name="peer_stats",
    )(x, g, wqt, keys)


K1_BLOCK = 2


def _peer_dense_kernel(x_ref, xnt_ref, rank2_ref, cnt1_ref, e1_ref, e2n_ref, u_ref, v_ref,
                       gf_ref, o_ref, acc_ref, a_ref, w_ref, *, final_norm):
    n_heads, n_keys, tt = cnt1_ref.shape[0], 2 * rank2_ref.shape[1], rank2_ref.shape[2]
    et = 2 * u_ref.shape[0]
    n_k1 = et // n_keys
    j = pl.program_id(1)
    n_j = pl.num_programs(1) - 2
    packed = (n_keys // BF16_ROWS, BF16_ROWS, LANES)

    def activation_matmul(slot):
        a_ref[slot] = jnp.dot(_unpack_rows(u_ref[...]), _unpack_rows(xnt_ref[...]),
                              preferred_element_type=F32)

    def value_matmul(slot):
        return lax.dot_general(w_ref[slot], _unpack_rows(v_ref[...]),
                               (((0,), (0,)), ((), ())), preferred_element_type=F32)

    def row_tile(ref, h, kk, cols):
        words = jnp.broadcast_to(ref[h, kk:kk + 1, cols], (SUBLANES, LANES))
        return _unpack_rows(words)[None]

    def build_weighted(slot):
        for kb in range(0, n_k1, K1_BLOCK):
            for c in range(tt // LANES):
                cols = slice(c * LANES, (c + 1) * LANES)
                g = [None] * K1_BLOCK
                for h in range(n_heads):
                    r2 = _unpack_rows(rank2_ref[h, :, cols]).reshape(packed)
                    e2 = _unpack_rows(e2n_ref[h, :, cols]).reshape(packed)
                    for q in range(K1_BLOCK):
                        kk = kb + q
                        sel = jnp.where(r2 < row_tile(cnt1_ref, h, kk, cols), e2,
                                        jnp.zeros_like(e2))
                        term = sel * row_tile(e1_ref, h, kk, cols)
                        g[q] = term if h == 0 else g[q] + term
                for q in range(K1_BLOCK):
                    rows = slice((kb + q) * n_keys, (kb + q + 1) * n_keys)
                    act = _gelu_tanh(a_ref[slot, rows, cols].astype(BF16))
                    w_ref[slot, rows, cols] = g[q].reshape(n_keys, LANES) * act

    @pl.when(j == 0)
    def _():
        acc_ref[...] = jnp.zeros_like(acc_ref)
        w_ref[...] = jnp.zeros_like(w_ref)
        activation_matmul(0)

    @pl.when((j >= 1) & (j <= n_j))
    def _():
        cur = j % 2
        activation_matmul(cur)
        build_weighted(1 - cur)
        acc_ref[...] += value_matmul(cur)

    @pl.when(j == n_j + 1)
    def _():
        y = x_ref[...] + (acc_ref[...] + value_matmul(j % 2))
        if final_norm:
            y = _rms(y, gf_ref[...])
        o_ref[...] = y


def _peer_dense(x, xnt, rank2, cnt1, e1, e2n, u_pk, v_pk, g_final, final_norm):
    n, d = x.shape
    tt = min(PEER_TOK_TILE, n)
    n_heads, n_keys, _ = cnt1.shape
    et = PEER_EXP_TILE
    n_j = 2 * u_pk.shape[0] // et
    n_k1 = et // n_keys
    last = n_j - 1
    packed_spec = pl.BlockSpec((n_heads, n_keys // 2, tt), lambda i, j: (0, 0, i))
    k1_spec = pl.BlockSpec((n_heads, n_k1, tt),
                           lambda i, j: (0, jnp.clip(j - 1, 0, last), i))
    return pl.pallas_call(
        functools.partial(_peer_dense_kernel, final_norm=final_norm),
        grid=(n // tt, n_j + 2),
        in_specs=[
            pl.BlockSpec((tt, d), lambda i, j: (i, 0)),
            pl.BlockSpec((d // 2, tt), lambda i, j: (0, i)),
            packed_spec, k1_spec, k1_spec, packed_spec,
            pl.BlockSpec((et // 2, d), lambda i, j: (jnp.minimum(j, last), 0)),
            pl.BlockSpec((et // 2, d), lambda i, j: (jnp.clip(j - 2, 0, last), 0)),
            pl.BlockSpec((1, d), lambda i, j: (0, 0)),
        ],
        out_specs=pl.BlockSpec((tt, d), lambda i, j: (i, 0)),
        out_shape=jax.ShapeDtypeStruct((n, d), F32),
        scratch_shapes=[
            pltpu.VMEM((tt, d), F32),
            pltpu.VMEM((2, et, tt), F32),
            pltpu.VMEM((2, et, tt), BF16),
        ],
        compiler_params=_cparams(("parallel", "arbitrary")),
        name="peer_dense",
    )(x, xnt, rank2, cnt1, e1, e2n, u_pk, v_pk, g_final)


def _peer(x, g, wqt, keys, u_pk, v_pk, g_final, final_norm):
    xnt, rank2, cnt1, e1, e2n = _peer_stats(x, g, wqt, keys)
    return _peer_dense(x, xnt, rank2, cnt1, e1, e2n, u_pk, v_pk, g_final, final_norm)


def _block_diag_groups(w):
    n_blocks, bs, _ = w.shape
    per = MXU_DIM // bs
    w4 = w.reshape(n_blocks // per, per, bs, bs)
    eye = jnp.eye(per, dtype=w.dtype)
    bd = jnp.einsum('gaij,ab->gaibj', w4, eye)
    return bd.reshape(n_blocks // per, MXU_DIM, MXU_DIM).astype(BF16)


def _prep_weights(p):
    row = lambda v: v.reshape(1, -1)
    depth = p['norm_mix'].shape[0]
    w = dict(p)
    w['rg_w_in'] = p['rg_w_in'][0].astype(BF16)
    w['rg_wa_bd'] = _block_diag_groups(p['rg_w_a'][0])
    w['rg_wi_bd'] = _block_diag_groups(p['rg_w_i'][0])
    w['rg_w_out'] = p['rg_w_out'][0].astype(BF16)
    w['cf_w_pw1'] = p['cf_w_pw1'][0].astype(BF16)
    w['cf_w_pw2'] = p['cf_w_pw2'][0].astype(BF16)
    w['peer_wqt'] = [p['peer_w_q'][i].T.astype(BF16) for i in range(depth)]
    n_keys = p['peer_sub_keys'].shape[-2]
    keys = p['peer_sub_keys'].at[:, :, 1].set(p['peer_sub_keys'][:, :, 1][:, :, _paired_order(n_keys)])
    w['peer_keys'] = [keys[i].reshape((-1,) + keys.shape[-2:]) for i in range(depth)]
    w['peer_u_pk'] = [_pack_row_groups_xla(p['peer_u'][i]) for i in range(depth)]
    w['peer_v_pk'] = [_pack_row_groups_xla(p['peer_v'][i]) for i in range(depth)]
    w['row'] = row
    return w


def _trunk(x_btd, rg_h0, rg_conv0, cf_conv0, w):
    nb, t, d = x_btd.shape
    row = w['row']
    x = jnp.transpose(x_btd, (1, 0, 2))
    n = t * nb
    zeros2d = jnp.zeros((1, 2 * d), F32)

    proj = _norm_matmul(x.reshape(n, d), row(w['norm_mix'][0]), w['rg_w_in'], zeros2d)
    x, h_last, rg_cs = _rg_block(
        x, proj.reshape(t, nb, 2 * d), rg_h0[0], jnp.transpose(rg_conv0[0], (1, 0, 2)),
        w['rg_conv_w'][0], row(w['rg_conv_b'][0]), w['rg_wa_bd'], row(w['rg_b_a'][0]),
        w['rg_wi_bd'], row(w['rg_b_i'][0]), row(w['rg_lambda'][0]), w['rg_w_out'])
    x = _peer(x.reshape(n, d), row(w['norm_ffn'][0]), w['peer_wqt'][0], w['peer_keys'][0],
              w['peer_u_pk'][0], w['peer_v_pk'][0], row(w['norm_final']), False)

    pp = _norm_matmul(x, row(w['norm_mix'][1]), w['cf_w_pw1'], row(w['cf_b_pw1'][0]))
    x, cf_cs = _cf_block(
        x.reshape(t, nb, d), pp.reshape(t, nb, 2 * d), jnp.transpose(cf_conv0[0], (1, 0, 2)),
        w['cf_dw_w'][0], row(w['cf_dw_b'][0]), row(w['cf_ln_g'][0]), row(w['cf_ln_b'][0]),
        w['cf_w_pw2'], row(w['cf_b_pw2'][0]))
    y = _peer(x.reshape(n, d), row(w['norm_ffn'][1]), w['peer_wqt'][1], w['peer_keys'][1],
              w['peer_u_pk'][1], w['peer_v_pk'][1], row(w['norm_final']), True)

    y = jnp.transpose(y.reshape(t, nb, d), (1, 0, 2))
    return (y, h_last[None], jnp.transpose(rg_cs, (1, 0, 2))[None],
            jnp.transpose(cf_cs, (1, 0, 2))[None])


def kernel(x_prompt, x_sample, state_rglru_h, state_rglru_conv, state_conformer_conv, norm_mix, norm_ffn, norm_final, rg_w_in, rg_conv_w, rg_conv_b, rg_w_a, rg_b_a, rg_w_i, rg_b_i, rg_lambda, rg_w_out, cf_w_pw1, cf_b_pw1, cf_dw_w, cf_dw_b, cf_ln_g, cf_ln_b, cf_w_pw2, cf_b_pw2, peer_w_q, peer_sub_keys, peer_u, peer_v):
    assert norm_mix.shape[0] == 2 and rg_w_in.shape[0] == 1 and cf_w_pw1.shape[0] == 1
    params = dict(norm_mix=norm_mix, norm_ffn=norm_ffn, norm_final=norm_final,
                  rg_w_in=rg_w_in, rg_conv_w=rg_conv_w, rg_conv_b=rg_conv_b,
                  rg_w_a=rg_w_a, rg_b_a=rg_b_a, rg_w_i=rg_w_i, rg_b_i=rg_b_i,
                  rg_lambda=rg_lambda, rg_w_out=rg_w_out,
                  cf_w_pw1=cf_w_pw1, cf_b_pw1=cf_b_pw1, cf_dw_w=cf_dw_w, cf_dw_b=cf_dw_b,
                  cf_ln_g=cf_ln_g, cf_ln_b=cf_ln_b, cf_w_pw2=cf_w_pw2, cf_b_pw2=cf_b_pw2,
                  peer_w_q=peer_w_q, peer_sub_keys=peer_sub_keys, peer_u=peer_u, peer_v=peer_v)
    w = _prep_weights(params)
    dt = x_prompt.dtype
    nbp, _, d = x_prompt.shape
    n_a, n_b = state_rglru_h.shape[0], state_conformer_conv.shape[0]
    h0_p = jnp.zeros((n_a, nbp, d), dt)
    rgc0_p = jnp.zeros((n_a, nbp) + state_rglru_conv.shape[2:], dt)
    cfc0_p = jnp.zeros((n_b, nbp) + state_conformer_conv.shape[2:], dt)
    y_p, p_h, p_rgc, p_cfc = _trunk(x_prompt, h0_p, rgc0_p, cfc0_p, w)
    y_s, s_h, s_rgc, s_cfc = _trunk(x_sample, state_rglru_h, state_rglru_conv,
                                    state_conformer_conv, w)
    return (y_p, y_s, p_h, p_rgc, p_cfc, s_h, s_rgc, s_cfc)
```

```python
import functools

import jax
import jax.numpy as jnp
import numpy as np
from jax import lax
from jax.experimental import pallas as pl
from jax.experimental.pallas import tpu as pltpu

F32 = jnp.float32
BF16 = jnp.bfloat16

EPS = 1e-6
RG_C = 8.0
TOPK = 16

LANES = 128
SUBLANES = 8
MXU_DIM = 256
VMEM_LIMIT = 56 * 1024 * 1024

ROW_TILE = 512
PEER_TOK_TILE = 512
PEER_EXP_TILE = 1024


def _rms(x, g):
    return x * lax.rsqrt(jnp.mean(x * x, axis=-1, keepdims=True) + EPS) * g


def _cparams(sem):
    return pltpu.CompilerParams(dimension_semantics=sem, vmem_limit_bytes=VMEM_LIMIT)


def _norm_matmul_kernel(x_ref, g_ref, w_ref, b_ref, o_ref):
    xn = _rms(x_ref[...], g_ref[...])
    o_ref[...] = jnp.dot(xn.astype(BF16), w_ref[...], preferred_element_type=F32) + b_ref[...]


def _norm_matmul(x, g, w, b):
    n, d = x.shape
    n_out = w.shape[1]
    tm = min(ROW_TILE, n)
    return pl.pallas_call(
        _norm_matmul_kernel,
        grid=(n // tm,),
        in_specs=[
            pl.BlockSpec((tm, d), lambda i: (i, 0)),
            pl.BlockSpec((1, d), lambda i: (0, 0)),
            pl.BlockSpec((d, n_out), lambda i: (0, 0)),
            pl.BlockSpec((1, n_out), lambda i: (0, 0)),
        ],
        out_specs=pl.BlockSpec((tm, n_out), lambda i: (i, 0)),
        out_shape=jax.ShapeDtypeStruct((n, n_out), F32),
        compiler_params=_cparams(("parallel",)),
        name="norm_matmul",
    )(x, g, w, b)


def _rg_kernel(x_ref, proj_ref, h0_ref, conv0_ref, cw_ref, cb_ref, wa_ref, ba_ref, wi_ref,
               bi_ref, lam_ref, wout_ref, o_ref, hlast_ref, cstate_ref,
               ext_ref, a_ref, u_ref, hcar_ref):
    tc, nb, d = x_ref.shape
    rows = tc * nb
    cw_taps = cw_ref.shape[0]
    hist = cw_taps - 1
    step = pl.program_id(1)

    @pl.when(step == 0)
    def _():
        ext_ref[0:hist] = conv0_ref[...]
        hcar_ref[...] = h0_ref[...]

    @pl.when(step > 0)
    def _():
        ext_ref[0:hist] = ext_ref[tc:tc + hist]

    ext_ref[hist:hist + tc] = proj_ref[:, :, d:]

    xc = cb_ref[...][None]
    for k in range(cw_taps):
        xc = xc + cw_ref[k:k + 1, :][None] * ext_ref[k:k + tc]
    xc2 = xc.reshape(rows, d)
    xcb = xc2.astype(BF16)

    def blockdiag(w_ref):
        n_grp = w_ref.shape[0]
        gw = d // n_grp
        outs = [jnp.dot(xcb[:, g * gw:(g + 1) * gw], w_ref[g], preferred_element_type=F32)
                for g in range(n_grp)]
        return jnp.concatenate(outs, axis=-1)

    r = jax.nn.sigmoid(blockdiag(wa_ref) + ba_ref[...])
    gi = jax.nn.sigmoid(blockdiag(wi_ref) + bi_ref[...])
    neg_lam = -lam_ref[...]
    softplus = jnp.maximum(neg_lam, 0.0) + jnp.log1p(jnp.exp(-jnp.abs(neg_lam)))
    log_a = (-RG_C) * r * softplus
    a = jnp.exp(log_a)
    u = jnp.sqrt(-jnp.tanh(log_a) * (a * a + 1.0)) * (gi * xc2)
    a_ref[...] = a.reshape(tc, nb, d)
    u_ref[...] = u.reshape(tc, nb, d)

    def scan_body(t, h):
        h = a_ref[t] * h + u_ref[t]
        u_ref[t] = h
        return h

    h_fin = lax.fori_loop(0, tc, scan_body, hcar_ref[...], unroll=min(tc, 8))
    hcar_ref[...] = h_fin

    gate = jax.nn.gelu(proj_ref[:, :, :d].reshape(rows, d))
    gated = (u_ref[...].reshape(rows, d) * gate).astype(BF16)
    y = jnp.dot(gated, wout_ref[...], preferred_element_type=F32)
    o_ref[...] = x_ref[...] + y.reshape(tc, nb, d)

    @pl.when(step == pl.num_programs(1) - 1)
    def _():
        hlast_ref[...] = h_fin
        cstate_ref[...] = ext_ref[tc:tc + hist]


def _seq_tiles(t, nb, max_bb):
    bb = min(nb, max_bb, max(SUBLANES, ROW_TILE // t))
    tc = min(t, ROW_TILE // bb)
    assert nb % bb == 0 and t % tc == 0 and bb % SUBLANES == 0
    return bb, tc


def _rg_block(x_tm, proj_tm, h0, conv0, cw, cb, wa_bd, ba, wi_bd, bi, lam, wout):
    t, nb, d = x_tm.shape
    bb, tc = _seq_tiles(t, nb, ROW_TILE)
    hist = cw.shape[0] - 1
    assert t == tc or tc >= hist
    const2 = lambda b, i: (0, 0)
    const3 = lambda b, i: (0, 0, 0)
    return pl.pallas_call(
        _rg_kernel,
        grid=(nb // bb, t // tc),
        in_specs=[
            pl.BlockSpec((tc, bb, d), lambda b, i: (i, b, 0)),
            pl.BlockSpec((tc, bb, 2 * d), lambda b, i: (i, b, 0)),
            pl.BlockSpec((bb, d), lambda b, i: (b, 0)),
            pl.BlockSpec((hist, bb, d), lambda b, i: (0, b, 0)),
            pl.BlockSpec(cw.shape, const2),
            pl.BlockSpec((1, d), const2),
            pl.BlockSpec(wa_bd.shape, const3),
            pl.BlockSpec((1, d), const2),
            pl.BlockSpec(wi_bd.shape, const3),
            pl.BlockSpec((1, d), const2),
            pl.BlockSpec((1, d), const2),
            pl.BlockSpec((d, d), const2),
        ],
        out_specs=[
            pl.BlockSpec((tc, bb, d), lambda b, i: (i, b, 0)),
            pl.BlockSpec((bb, d), lambda b, i: (b, 0)),
            pl.BlockSpec((hist, bb, d), lambda b, i: (0, b, 0)),
        ],
        out_shape=[
            jax.ShapeDtypeStruct((t, nb, d), F32),
            jax.ShapeDtypeStruct((nb, d), F32),
            jax.ShapeDtypeStruct((hist, nb, d), F32),
        ],
        scratch_shapes=[
            pltpu.VMEM((tc + hist, bb, d), F32),
            pltpu.VMEM((tc, bb, d), F32),
            pltpu.VMEM((tc, bb, d), F32),
            pltpu.VMEM((bb, d), F32),
        ],
        compiler_params=_cparams(("parallel", "arbitrary")),
        name="rg_block",
    )(x_tm, proj_tm, h0, conv0, cw, cb, wa_bd, ba, wi_bd, bi, lam, wout)


CONV_T_SUB = 4
CF_MAX_BATCH_ROWS = 32


def _cf_kernel(x_ref, p_ref, conv0_ref, dw_ref, dwb_ref, lng_ref, lnb_ref, w2_ref, b2_ref,
               o_ref, cstate_ref, ext_ref, c_ref, wb_ref):
    tc, nb, d = x_ref.shape
    rows = tc * nb
    taps = dw_ref.shape[0]
    hist = taps - 1
    step = pl.program_id(1)

    @pl.when(step == 0)
    def _():
        ext_ref[0:hist] = conv0_ref[...]
        for k in range(taps):
            wb_ref[k] = jnp.broadcast_to(dw_ref[k:k + 1, :], (SUBLANES, d))

    @pl.when(step > 0)
    def _():
        ext_ref[0:hist] = ext_ref[tc:tc + hist]

    ext_ref[hist:hist + tc] = p_ref[:, :, :d] * jax.nn.sigmoid(p_ref[:, :, d:])

    n_b = nb // SUBLANES
    bias = jnp.broadcast_to(dwb_ref[...], (SUBLANES, d))[None]

    def conv_body(idx, carry):
        t0 = (idx // n_b) * CONV_T_SUB
        b0 = pl.multiple_of((idx % n_b) * SUBLANES, SUBLANES)
        acc = jnp.broadcast_to(bias, (CONV_T_SUB, SUBLANES, d))
        for k in range(taps):
            acc = acc + wb_ref[k][None] * ext_ref[pl.ds(t0 + k, CONV_T_SUB), pl.ds(b0, SUBLANES), :]
        c_ref[pl.ds(t0, CONV_T_SUB), pl.ds(b0, SUBLANES), :] = acc
        return carry

    lax.fori_loop(0, (tc // CONV_T_SUB) * n_b, conv_body, 0)

    c = c_ref[...].reshape(rows, d)
    mu = jnp.mean(c, axis=-1, keepdims=True)
    cc = c - mu
    var = jnp.mean(cc * cc, axis=-1, keepdims=True)
    y = cc * lax.rsqrt(var + EPS) * lng_ref[...] + lnb_ref[...]
    y = jax.nn.silu(y).astype(BF16)
    out = jnp.dot(y, w2_ref[...], preferred_element_type=F32) + b2_ref[...]
    o_ref[...] = x_ref[...] + out.reshape(tc, nb, d)

    @pl.when(step == pl.num_programs(1) - 1)
    def _():
        cstate_ref[...] = ext_ref[tc:tc + hist]


def _cf_block(x_tm, p_tm, conv0, dw, dwb, lng, lnb, w2, b2):
    t, nb, d = x_tm.shape
    bb, tc = _seq_tiles(t, nb, CF_MAX_BATCH_ROWS)
    taps = dw.shape[0]
    hist = taps - 1
    assert tc % CONV_T_SUB == 0
    assert t == tc or tc >= hist
    const2 = lambda b, i: (0, 0)
    return pl.pallas_call(
        _cf_kernel,
        grid=(nb // bb, t // tc),
        in_specs=[
            pl.BlockSpec((tc, bb, d), lambda b, i: (i, b, 0)),
            pl.BlockSpec((tc, bb, 2 * d), lambda b, i: (i, b, 0)),
            pl.BlockSpec((hist, bb, d), lambda b, i: (0, b, 0)),
            pl.BlockSpec(dw.shape, const2),
            pl.BlockSpec((1, d), const2),
            pl.BlockSpec((1, d), const2),
            pl.BlockSpec((1, d), const2),
            pl.BlockSpec((d, d), const2),
            pl.BlockSpec((1, d), const2),
        ],
        out_specs=[
            pl.BlockSpec((tc, bb, d), lambda b, i: (i, b, 0)),
            pl.BlockSpec((hist, bb, d), lambda b, i: (0, b, 0)),
        ],
        out_shape=[
            jax.ShapeDtypeStruct((t, nb, d), F32),
            jax.ShapeDtypeStruct((hist, nb, d), F32),
        ],
        scratch_shapes=[
            pltpu.VMEM((tc + hist, bb, d), F32),
            pltpu.VMEM((tc, bb, d), F32),
            pltpu.VMEM((taps, SUBLANES, d), F32),
        ],
        compiler_params=_cparams(("parallel", "arbitrary")),
        name="cf_block",
    )(x_tm, p_tm, conv0, dw, dwb, lng, lnb, w2, b2)


U32 = jnp.uint32
BF16_ROWS = 16


def _pack_rows(v_bf16):
    return pltpu.bitcast(v_bf16, U32)


def _unpack_rows(words):
    return pltpu.bitcast(words, BF16)


def _dup_bf16_words(v):
    hi = pltpu.bitcast(v.astype(BF16).astype(F32), U32)
    return hi | (hi >> 16)


def _pack_row_groups_xla(w):
    r, c = w.shape
    groups = w.reshape(r // BF16_ROWS, 2, SUBLANES, c)

    def high_half_bits(x):
        return lax.bitcast_convert_type(x.astype(BF16).astype(F32), U32)

    words = (high_half_bits(groups[:, 0]) >> 16) | high_half_bits(groups[:, 1])
    return words.reshape(r // 2, c)


def _paired_order(n):
    pos = np.arange(n)
    grp, q = pos // BF16_ROWS, pos % BF16_ROWS
    return grp * BF16_ROWS + (q % 2) * SUBLANES + q // 2


def _gelu_tanh(x):
    k1 = 2.0 * 0.7978845608028654
    k2 = k1 * 0.044715
    neg_z = x * (x * x * (-k2) + (-k1))
    return x / (1.0 + jnp.exp(neg_z))


def _merge_exchange_network(n):
    pairs = []
    p = 1
    while p < n:
        k = p
        while k >= 1:
            for j in range(k % p, n - k, 2 * k):
                for i in range(min(k, n - j - k)):
                    if (i + j) // (2 * p) == (i + j + k) // (2 * p):
                        pairs.append((i + j, i + j + k))
            k //= 2
        p *= 2
    return pairs


def _sorted_topk(s, sv_ref, want_rank):
    n_tiles = s.shape[0] // SUBLANES
    v = [s[i * SUBLANES:(i + 1) * SUBLANES, :] for i in range(n_tiles)]
    for a, b in _merge_exchange_network(n_tiles):
        v[a], v[b] = jnp.maximum(v[a], v[b]), jnp.minimum(v[a], v[b])
    tops = []
    for t in range(TOPK):
        m = jnp.max(v[0], axis=0, keepdims=True)
        sv_ref[t:t + 1, :] = m
        tops.append(m)
        if t + 1 < TOPK:
            hit = v[0] == m
            for i in range(min(TOPK - 1 - t, n_tiles - 1)):
                v[i] = jnp.where(hit, v[i + 1], v[i])
    if not want_rank:
        return None
    rank = jnp.zeros(s.shape, F32)
    for m in tops:
        rank = rank + jnp.where(s < m, 1.0, 0.0)
    return rank


def _peer_stats_kernel(x_ref, g_ref, wqt_ref, keys_ref, xnt_ref, rank2_ref, cnt1_ref, e1_ref,
                       e2n_ref, qt_ref, s_ref, sv1_ref, sv2_ref):
    n_heads = rank2_ref.shape[0]
    xn = _rms(x_ref[...], g_ref[...])
    xnt = xn.T.astype(BF16)
    xnt_ref[...] = _pack_rows(xnt)
    qt_ref[...] = jnp.dot(wqt_ref[...], xnt, preferred_element_type=F32)
    d_half = qt_ref.shape[0] // (2 * n_heads)

    def head_body(h, carry):
        q1 = qt_ref[pl.ds(pl.multiple_of(h * 2 * d_half, d_half), d_half), :]
        q2 = qt_ref[pl.ds(pl.multiple_of(h * 2 * d_half + d_half, d_half), d_half), :]
        s_ref[0] = jnp.dot(keys_ref[2 * h], q1, preferred_element_type=F32)
        s_ref[1] = jnp.dot(keys_ref[2 * h + 1], q2, preferred_element_type=F32)
        for c in range(s_ref.shape[2] // LANES):
            lane_tile_stats(h, slice(c * LANES, (c + 1) * LANES))
        return carry

    def lane_tile_stats(h, cols):
        s1 = s_ref[0, :, cols]
        s2 = s_ref[1, :, cols]
        _sorted_topk(s1, sv1_ref, False)
        rank2 = _sorted_topk(s2, sv2_ref, True)
        sv1 = sv1_ref[...]
        sv2 = sv2_ref[...]
        half = TOPK // 2
        cand = [sv1[0:1] + sv2]
        cand += [sv1[i:i + 1] + sv2[0:half] for i in range(1, half)]
        cand += [sv1[half:] + sv2[0:1]]
        work = jnp.concatenate(cand, axis=0)
        tau = None
        for i in range(TOPK):
            tau = jnp.max(work, axis=0, keepdims=True)
            if i + 1 < TOPK:
                work = jnp.where(work >= tau, -jnp.inf, work)
        e2v = jnp.exp(sv2 - sv2[0:1])
        cnt_rows = jnp.zeros(sv1.shape, F32)
        mass = jnp.zeros(sv1.shape, F32)
        prefix = jnp.zeros_like(tau)
        for j in range(TOPK):
            prefix = prefix + e2v[j:j + 1]
            cond = (sv1 + sv2[j:j + 1]) >= tau
            cnt_rows = cnt_rows + jnp.where(cond, 1.0, 0.0)
            mass = jnp.where(cond, prefix, mass)
        z = jnp.sum(jnp.exp(sv1 - sv1[0:1]) * mass, axis=0, keepdims=True)
        cnt1 = jnp.zeros(s1.shape, F32)
        for i in range(TOPK):
            cnt1 = jnp.where(s1 == sv1[i:i + 1], cnt_rows[i:i + 1], cnt1)
        rank2_ref[h, :, cols] = _pack_rows(rank2.astype(BF16))
        cnt1_ref[h, :, cols] = _dup_bf16_words(cnt1)
        e1_ref[h, :, cols] = _dup_bf16_words(jnp.exp(s1 - sv1[0:1]))
        e2n_ref[h, :, cols] = _pack_rows((jnp.exp(s2 - sv2[0:1]) / z).astype(BF16))

    lax.fori_loop(0, n_heads, head_body, 0)


def _peer_stats(x, g, wqt, keys):
    n, d = x.shape
    tt = min(PEER_TOK_TILE, n)
    n_heads = keys.shape[0] // 2
    n_keys = keys.shape[1]
    row_spec = pl.BlockSpec((n_heads, n_keys, tt), lambda i: (0, 0, i))
    packed_spec = pl.BlockSpec((n_heads, n_keys // 2, tt), lambda i: (0, 0, i))
    row_words = jax.ShapeDtypeStruct((n_heads, n_keys, n), U32)
    packed_words = jax.ShapeDtypeStruct((n_heads, n_keys // 2, n), U32)
    return pl.pallas_call(
        _peer_stats_kernel,
        grid=(n // tt,),
        in_specs=[
            pl.BlockSpec((tt, d), lambda i: (i, 0)),
            pl.BlockSpec((1, d), lambda i: (0, 0)),
            pl.BlockSpec(wqt.shape, lambda i: (0, 0)),
            pl.BlockSpec(keys.shape, lambda i: (0, 0, 0)),
        ],
        out_specs=[pl.BlockSpec((d // 2, tt), lambda i: (0, i)), packed_spec, row_spec, row_spec,
                   packed_spec],
        out_shape=[jax.ShapeDtypeStruct((d // 2, n), U32), packed_words, row_words, row_words,
                   packed_words],
        scratch_shapes=[
            pltpu.VMEM((wqt.shape[0], tt), F32),
            pltpu.VMEM((2, n_keys, tt), F32),
            pltpu.VMEM((TOPK, LANES), F32),
            pltpu.VMEM((TOPK, LANES), F32),
        ],
        compiler_params=_cparams(("parallel",)),
        name="peer_stats",
    )(x, g, wqt, keys)


K1_BLOCK = 8


def _peer_dense_kernel(x_ref, xnt_ref, rank2_ref, cnt1_ref, e1_ref, e2n_ref, u_ref, v_ref,
                       gf_ref, o_ref, acc_ref, a_ref, w_ref, *, n_j, final_norm):
    n_heads, n_keys, tt = cnt1_ref.shape[0], 2 * rank2_ref.shape[1], rank2_ref.shape[2]
    et = 2 * u_ref.shape[0]
    n_k1 = et // n_keys
    j = pl.program_id(1)
    packed = (n_keys // BF16_ROWS, BF16_ROWS, LANES)

    d = v_ref.shape[1]
    a_slices = [slice(s * MXU_DIM, (s + 1) * MXU_DIM) for s in range(tt // MXU_DIM)]
    v_slices = [slice(s * MXU_DIM, (s + 1) * MXU_DIM) for s in range(d // MXU_DIM)]

    def activation_matmul(slot, cols):
        a_ref[slot, :, cols] = jnp.dot(_unpack_rows(u_ref[...]), _unpack_rows(xnt_ref[:, cols]),
                                       preferred_element_type=F32)

    def value_matmul(slot, cols):
        return lax.dot_general(w_ref[slot], _unpack_rows(v_ref[:, cols]),
                               (((0,), (0,)), ((), ())), preferred_element_type=F32)

    def row_tile(ref, h, kk, cols):
        words = jnp.broadcast_to(ref[h, kk:kk + 1, cols], (SUBLANES, LANES))
        return _unpack_rows(words)[None]

    units = [(kb, c) for kb in range(0, n_k1, K1_BLOCK) for c in range(tt // LANES)]

    def build_weighted(slot, unit_ids):
        for kb, c in [units[n] for n in unit_ids]:
            cols = slice(c * LANES, (c + 1) * LANES)
            g = [None] * K1_BLOCK
            for h in range(n_heads):
                r2 = _unpack_rows(rank2_ref[h, :, cols]).reshape(packed)
                e2 = _unpack_rows(e2n_ref[h, :, cols]).reshape(packed)
                for q in range(K1_BLOCK):
                    kk = kb + q
                    sel = jnp.where(r2 < row_tile(cnt1_ref, h, kk, cols), e2, jnp.zeros_like(e2))
                    term = sel * row_tile(e1_ref, h, kk, cols)
                    g[q] = term if h == 0 else g[q] + term
            for q in range(K1_BLOCK):
                rows = slice((kb + q) * n_keys, (kb + q + 1) * n_keys)
                act = _gelu_tanh(a_ref[slot, rows, cols].astype(BF16))
                w_ref[slot, rows, cols] = g[q].reshape(n_keys, LANES) * act

    @pl.when(j == 0)
    def _():
        acc_ref[...] = jnp.zeros_like(acc_ref)
        w_ref[...] = jnp.zeros_like(w_ref)
        for cols in a_slices:
            activation_matmul(0, cols)

    n_groups = len(a_slices) + len(v_slices)
    lead = min(len(units), n_groups)
    bounds = [min(g, lead) * len(units) // lead for g in range(n_groups + 1)]
    assert bounds[-1] == len(units)

    for cur in range(2):
        @pl.when((j >= 1) & (j <= n_j) & (j % 2 == cur))
        def _():
            for g in range(n_groups):
                build_weighted(1 - cur, range(bounds[g], bounds[g + 1]))
                if g < len(a_slices):
                    activation_matmul(cur, a_slices[g])
                else:
                    cols = v_slices[g - len(a_slices)]
                    acc_ref[:, cols] += value_matmul(cur, cols)

    @pl.when(j == n_j + 1)
    def _():
        for cols in v_slices:
            y = x_ref[:, cols] + acc_ref[:, cols] + value_matmul((n_j + 1) % 2, cols)
            if final_norm:
                acc_ref[:, cols] = y
            else:
                o_ref[:, cols] = y
        if final_norm:
            o_ref[...] = _rms(acc_ref[...], gf_ref[...])


def _peer_dense(x, xnt, rank2, cnt1, e1, e2n, u_pk, v_pk, g_final, final_norm):
    n, d = x.shape
    tt = min(PEER_TOK_TILE, n)
    n_heads, n_keys, _ = cnt1.shape
    et = PEER_EXP_TILE
    n_j = 2 * u_pk.shape[0] // et
    n_k1 = et // n_keys
    last = n_j - 1
    packed_spec = pl.BlockSpec((n_heads, n_keys // 2, tt), lambda i, j: (0, 0, i))
    k1_spec = pl.BlockSpec((n_heads, n_k1, tt),
                           lambda i, j: (0, jnp.clip(j - 1, 0, last), i))
    return pl.pallas_call(
        functools.partial(_peer_dense_kernel, n_j=n_j, final_norm=final_norm),
        grid=(n // tt, n_j + 2),
        in_specs=[
            pl.BlockSpec((tt, d), lambda i, j: (i, 0)),
            pl.BlockSpec((d // 2, tt), lambda i, j: (0, i)),
            packed_spec, k1_spec, k1_spec, packed_spec,
            pl.BlockSpec((et // 2, d), lambda i, j: (jnp.minimum(j, last), 0)),
            pl.BlockSpec((et // 2, d), lambda i, j: (jnp.clip(j - 2, 0, last), 0)),
            pl.BlockSpec((1, d), lambda i, j: (0, 0)),
        ],
        out_specs=pl.BlockSpec((tt, d), lambda i, j: (i, 0)),
        out_shape=jax.ShapeDtypeStruct((n, d), F32),
        scratch_shapes=[
            pltpu.VMEM((tt, d), F32),
            pltpu.VMEM((2, et, tt), F32),
            pltpu.VMEM((2, et, tt), BF16),
        ],
        compiler_params=_cparams(("parallel", "arbitrary")),
        name="peer_dense",
    )(x, xnt, rank2, cnt1, e1, e2n, u_pk, v_pk, g_final)


def _peer(x, g, wqt, keys, u_pk, v_pk, g_final, final_norm):
    xnt, rank2, cnt1, e1, e2n = _peer_stats(x, g, wqt, keys)
    return _peer_dense(x, xnt, rank2, cnt1, e1, e2n, u_pk, v_pk, g_final, final_norm)


def _block_diag_groups(w):
    n_blocks, bs, _ = w.shape
    per = MXU_DIM // bs
    w4 = w.reshape(n_blocks // per, per, bs, bs)
    eye = jnp.eye(per, dtype=w.dtype)
    bd = jnp.einsum('gaij,ab->gaibj', w4, eye)
    return bd.reshape(n_blocks // per, MXU_DIM, MXU_DIM).astype(BF16)


def _prep_weights(p):
    row = lambda v: v.reshape(1, -1)
    depth = p['norm_mix'].shape[0]
    w = dict(p)
    w['rg_w_in'] = p['rg_w_in'][0].astype(BF16)
    w['rg_wa_bd'] = _block_diag_groups(p['rg_w_a'][0])
    w['rg_wi_bd'] = _block_diag_groups(p['rg_w_i'][0])
    w['rg_w_out'] = p['rg_w_out'][0].astype(BF16)
    w['cf_w_pw1'] = p['cf_w_pw1'][0].astype(BF16)
    w['cf_w_pw2'] = p['cf_w_pw2'][0].astype(BF16)
    w['peer_wqt'] = [p['peer_w_q'][i].T.astype(BF16) for i in range(depth)]
    n_keys = p['peer_sub_keys'].shape[-2]
    keys = p['peer_sub_keys'].at[:, :, 1].set(p['peer_sub_keys'][:, :, 1][:, :, _paired_order(n_keys)])
    w['peer_keys'] = [keys[i].reshape((-1,) + keys.shape[-2:]) for i in range(depth)]
    w['peer_u_pk'] = [_pack_row_groups_xla(p['peer_u'][i]) for i in range(depth)]
    w['peer_v_pk'] = [_pack_row_groups_xla(p['peer_v'][i]) for i in range(depth)]
    w['row'] = row
    return w


def _trunk(x_btd, rg_h0, rg_conv0, cf_conv0, w):
    nb, t, d = x_btd.shape
    row = w['row']
    x = jnp.transpose(x_btd, (1, 0, 2))
    n = t * nb
    zeros2d = jnp.zeros((1, 2 * d), F32)

    proj = _norm_matmul(x.reshape(n, d), row(w['norm_mix'][0]), w['rg_w_in'], zeros2d)
    x, h_last, rg_cs = _rg_block(
        x, proj.reshape(t, nb, 2 * d), rg_h0[0], jnp.transpose(rg_conv0[0], (1, 0, 2)),
        w['rg_conv_w'][0], row(w['rg_conv_b'][0]), w['rg_wa_bd'], row(w['rg_b_a'][0]),
        w['rg_wi_bd'], row(w['rg_b_i'][0]), row(w['rg_lambda'][0]), w['rg_w_out'])
    x = _peer(x.reshape(n, d), row(w['norm_ffn'][0]), w['peer_wqt'][0], w['peer_keys'][0],
              w['peer_u_pk'][0], w['peer_v_pk'][0], row(w['norm_final']), False)

    pp = _norm_matmul(x, row(w['norm_mix'][1]), w['cf_w_pw1'], row(w['cf_b_pw1'][0]))
    x, cf_cs = _cf_block(
        x.reshape(t, nb, d), pp.reshape(t, nb, 2 * d), jnp.transpose(cf_conv0[0], (1, 0, 2)),
        w['cf_dw_w'][0], row(w['cf_dw_b'][0]), row(w['cf_ln_g'][0]), row(w['cf_ln_b'][0]),
        w['cf_w_pw2'], row(w['cf_b_pw2'][0]))
    y = _peer(x.reshape(n, d), row(w['norm_ffn'][1]), w['peer_wqt'][1], w['peer_keys'][1],
              w['peer_u_pk'][1], w['peer_v_pk'][1], row(w['norm_final']), True)

    y = jnp.transpose(y.reshape(t, nb, d), (1, 0, 2))
    return (y, h_last[None], jnp.transpose(rg_cs, (1, 0, 2))[None],
            jnp.transpose(cf_cs, (1, 0, 2))[None])


def kernel(x_prompt, x_sample, state_rglru_h, state_rglru_conv, state_conformer_conv, norm_mix, norm_ffn, norm_final, rg_w_in, rg_conv_w, rg_conv_b, rg_w_a, rg_b_a, rg_w_i, rg_b_i, rg_lambda, rg_w_out, cf_w_pw1, cf_b_pw1, cf_dw_w, cf_dw_b, cf_ln_g, cf_ln_b, cf_w_pw2, cf_b_pw2, peer_w_q, peer_sub_keys, peer_u, peer_v):
    assert norm_mix.shape[0] == 2 and rg_w_in.shape[0] == 1 and cf_w_pw1.shape[0] == 1
    params = dict(norm_mix=norm_mix, norm_ffn=norm_ffn, norm_final=norm_final,
                  rg_w_in=rg_w_in, rg_conv_w=rg_conv_w, rg_conv_b=rg_conv_b,
                  rg_w_a=rg_w_a, rg_b_a=rg_b_a, rg_w_i=rg_w_i, rg_b_i=rg_b_i,
                  rg_lambda=rg_lambda, rg_w_out=rg_w_out,
                  cf_w_pw1=cf_w_pw1, cf_b_pw1=cf_b_pw1, cf_dw_w=cf_dw_w, cf_dw_b=cf_dw_b,
                  cf_ln_g=cf_ln_g, cf_ln_b=cf_ln_b, cf_w_pw2=cf_w_pw2, cf_b_pw2=cf_b_pw2,
                  peer_w_q=peer_w_q, peer_sub_keys=peer_sub_keys, peer_u=peer_u, peer_v=peer_v)
    w = _prep_weights(params)
    dt = x_prompt.dtype
    nbp, _, d = x_prompt.shape
    n_a, n_b = state_rglru_h.shape[0], state_conformer_conv.shape[0]
    h0_p = jnp.zeros((n_a, nbp, d), dt)
    rgc0_p = jnp.zeros((n_a, nbp) + state_rglru_conv.shape[2:], dt)
    cfc0_p = jnp.zeros((n_b, nbp) + state_conformer_conv.shape[2:], dt)
    y_p, p_h, p_rgc, p_cfc = _trunk(x_prompt, h0_p, rgc0_p, cfc0_p, w)
    y_s, s_h, s_rgc, s_cfc = _trunk(x_sample, state_rglru_h, state_rglru_conv,
                                    state_conformer_conv, w)
    return (y_p, y_s, p_h, p_rgc, p_cfc, s_h, s_rgc, s_cfc)
```

```python
import functools

import jax
import jax.numpy as jnp
import numpy as np
from jax import lax
from jax.experimental import pallas as pl
from jax.experimental.pallas import tpu as pltpu

F32 = jnp.float32
BF16 = jnp.bfloat16

EPS = 1e-6
RG_C = 8.0
TOPK = 16

LANES = 128
SUBLANES = 8
MXU_DIM = 256
VMEM_LIMIT = 56 * 1024 * 1024

ROW_TILE = 512
PEER_TOK_TILE = 512
PEER_EXP_TILE = 1024


def _rms(x, g):
    return x * lax.rsqrt(jnp.mean(x * x, axis=-1, keepdims=True) + EPS) * g


def _cparams(sem):
    return pltpu.CompilerParams(dimension_semantics=sem, vmem_limit_bytes=VMEM_LIMIT)


def _norm_matmul_kernel(x_ref, g_ref, w_ref, b_ref, o_ref):
    xn = _rms(x_ref[...], g_ref[...])
    o_ref[...] = jnp.dot(xn.astype(BF16), w_ref[...], preferred_element_type=F32) + b_ref[...]


def _norm_matmul(x, g, w, b):
    n, d = x.shape
    n_out = w.shape[1]
    tm = min(ROW_TILE, n)
    return pl.pallas_call(
        _norm_matmul_kernel,
        grid=(n // tm,),
        in_specs=[
            pl.BlockSpec((tm, d), lambda i: (i, 0)),
            pl.BlockSpec((1, d), lambda i: (0, 0)),
            pl.BlockSpec((d, n_out), lambda i: (0, 0)),
            pl.BlockSpec((1, n_out), lambda i: (0, 0)),
        ],
        out_specs=pl.BlockSpec((tm, n_out), lambda i: (i, 0)),
        out_shape=jax.ShapeDtypeStruct((n, n_out), F32),
        compiler_params=_cparams(("parallel",)),
        name="norm_matmul",
    )(x, g, w, b)


def _rg_kernel(x_ref, proj_ref, h0_ref, conv0_ref, cw_ref, cb_ref, wa_ref, ba_ref, wi_ref,
               bi_ref, lam_ref, wout_ref, o_ref, hlast_ref, cstate_ref,
               ext_ref, a_ref, u_ref, hcar_ref):
    tc, nb, d = x_ref.shape
    rows = tc * nb
    cw_taps = cw_ref.shape[0]
    hist = cw_taps - 1
    step = pl.program_id(1)

    @pl.when(step == 0)
    def _():
        ext_ref[0:hist] = conv0_ref[...]
        hcar_ref[...] = h0_ref[...]

    @pl.when(step > 0)
    def _():
        ext_ref[0:hist] = ext_ref[tc:tc + hist]

    ext_ref[hist:hist + tc] = proj_ref[:, :, d:]

    xc = cb_ref[...][None]
    for k in range(cw_taps):
        xc = xc + cw_ref[k:k + 1, :][None] * ext_ref[k:k + tc]
    xc2 = xc.reshape(rows, d)
    xcb = xc2.astype(BF16)

    def blockdiag(w_ref):
        n_grp = w_ref.shape[0]
        gw = d // n_grp
        outs = [jnp.dot(xcb[:, g * gw:(g + 1) * gw], w_ref[g], preferred_element_type=F32)
                for g in range(n_grp)]
        return jnp.concatenate(outs, axis=-1)

    r = jax.nn.sigmoid(blockdiag(wa_ref) + ba_ref[...])
    gi = jax.nn.sigmoid(blockdiag(wi_ref) + bi_ref[...])
    neg_lam = -lam_ref[...]
    softplus = jnp.maximum(neg_lam, 0.0) + jnp.log1p(jnp.exp(-jnp.abs(neg_lam)))
    log_a = (-RG_C) * r * softplus
    a = jnp.exp(log_a)
    u = jnp.sqrt(-jnp.tanh(log_a) * (a * a + 1.0)) * (gi * xc2)
    a_ref[...] = a.reshape(tc, nb, d)
    u_ref[...] = u.reshape(tc, nb, d)

    def scan_body(t, h):
        h = a_ref[t] * h + u_ref[t]
        u_ref[t] = h
        return h

    h_fin = lax.fori_loop(0, tc, scan_body, hcar_ref[...], unroll=min(tc, 8))
    hcar_ref[...] = h_fin

    gate = jax.nn.gelu(proj_ref[:, :, :d].reshape(rows, d))
    gated = (u_ref[...].reshape(rows, d) * gate).astype(BF16)
    y = jnp.dot(gated, wout_ref[...], preferred_element_type=F32)
    o_ref[...] = x_ref[...] + y.reshape(tc, nb, d)

    @pl.when(step == pl.num_programs(1) - 1)
    def _():
        hlast_ref[...] = h_fin
        cstate_ref[...] = ext_ref[tc:tc + hist]


def _seq_tiles(t, nb, max_bb):
    bb = min(nb, max_bb, max(SUBLANES, ROW_TILE // t))
    tc = min(t, ROW_TILE // bb)
    assert nb % bb == 0 and t % tc == 0 and bb % SUBLANES == 0
    return bb, tc


def _rg_block(x_tm, proj_tm, h0, conv0, cw, cb, wa_bd, ba, wi_bd, bi, lam, wout):
    t, nb, d = x_tm.shape
    bb, tc = _seq_tiles(t, nb, ROW_TILE)
    hist = cw.shape[0] - 1
    assert t == tc or tc >= hist
    const2 = lambda b, i: (0, 0)
    const3 = lambda b, i: (0, 0, 0)
    return pl.pallas_call(
        _rg_kernel,
        grid=(nb // bb, t // tc),
        in_specs=[
            pl.BlockSpec((tc, bb, d), lambda b, i: (i, b, 0)),
            pl.BlockSpec((tc, bb, 2 * d), lambda b, i: (i, b, 0)),
            pl.BlockSpec((bb, d), lambda b, i: (b, 0)),
            pl.BlockSpec((hist, bb, d), lambda b, i: (0, b, 0)),
            pl.BlockSpec(cw.shape, const2),
            pl.BlockSpec((1, d), const2),
            pl.BlockSpec(wa_bd.shape, const3),
            pl.BlockSpec((1, d), const2),
            pl.BlockSpec(wi_bd.shape, const3),
            pl.BlockSpec((1, d), const2),
            pl.BlockSpec((1, d), const2),
            pl.BlockSpec((d, d), const2),
        ],
        out_specs=[
            pl.BlockSpec((tc, bb, d), lambda b, i: (i, b, 0)),
            pl.BlockSpec((bb, d), lambda b, i: (b, 0)),
            pl.BlockSpec((hist, bb, d), lambda b, i: (0, b, 0)),
        ],
        out_shape=[
            jax.ShapeDtypeStruct((t, nb, d), F32),
            jax.ShapeDtypeStruct((nb, d), F32),
            jax.ShapeDtypeStruct((hist, nb, d), F32),
        ],
        scratch_shapes=[
            pltpu.VMEM((tc + hist, bb, d), F32),
            pltpu.VMEM((tc, bb, d), F32),
            pltpu.VMEM((tc, bb, d), F32),
            pltpu.VMEM((bb, d), F32),
        ],
        compiler_params=_cparams(("parallel", "arbitrary")),
        name="rg_block",
    )(x_tm, proj_tm, h0, conv0, cw, cb, wa_bd, ba, wi_bd, bi, lam, wout)


CONV_T_SUB = 4
CF_MAX_BATCH_ROWS = 32


def _cf_kernel(x_ref, p_ref, conv0_ref, dw_ref, dwb_ref, lng_ref, lnb_ref, w2_ref, b2_ref,
               o_ref, cstate_ref, ext_ref, c_ref, wb_ref):
    tc, nb, d = x_ref.shape
    rows = tc * nb
    taps = dw_ref.shape[0]
    hist = taps - 1
    step = pl.program_id(1)

    @pl.when(step == 0)
    def _():
        ext_ref[0:hist] = conv0_ref[...]
        for k in range(taps):
            wb_ref[k] = jnp.broadcast_to(dw_ref[k:k + 1, :], (SUBLANES, d))

    @pl.when(step > 0)
    def _():
        ext_ref[0:hist] = ext_ref[tc:tc + hist]

    ext_ref[hist:hist + tc] = p_ref[:, :, :d] * jax.nn.sigmoid(p_ref[:, :, d:])

    n_b = nb // SUBLANES
    bias = jnp.broadcast_to(dwb_ref[...], (SUBLANES, d))[None]

    def conv_body(idx, carry):
        t0 = (idx // n_b) * CONV_T_SUB
        b0 = pl.multiple_of((idx % n_b) * SUBLANES, SUBLANES)
        acc = jnp.broadcast_to(bias, (CONV_T_SUB, SUBLANES, d))
        for k in range(taps):
            acc = acc + wb_ref[k][None] * ext_ref[pl.ds(t0 + k, CONV_T_SUB), pl.ds(b0, SUBLANES), :]
        c_ref[pl.ds(t0, CONV_T_SUB), pl.ds(b0, SUBLANES), :] = acc
        return carry

    lax.fori_loop(0, (tc // CONV_T_SUB) * n_b, conv_body, 0)

    c = c_ref[...].reshape(rows, d)
    mu = jnp.mean(c, axis=-1, keepdims=True)
    cc = c - mu
    var = jnp.mean(cc * cc, axis=-1, keepdims=True)
    y = cc * lax.rsqrt(var + EPS) * lng_ref[...] + lnb_ref[...]
    y = jax.nn.silu(y).astype(BF16)
    out = jnp.dot(y, w2_ref[...], preferred_element_type=F32) + b2_ref[...]
    o_ref[...] = x_ref[...] + out.reshape(tc, nb, d)

    @pl.when(step == pl.num_programs(1) - 1)
    def _():
        cstate_ref[...] = ext_ref[tc:tc + hist]


def _cf_block(x_tm, p_tm, conv0, dw, dwb, lng, lnb, w2, b2):
    t, nb, d = x_tm.shape
    bb, tc = _seq_tiles(t, nb, CF_MAX_BATCH_ROWS)
    taps = dw.shape[0]
    hist = taps - 1
    assert tc % CONV_T_SUB == 0
    assert t == tc or tc >= hist
    const2 = lambda b, i: (0, 0)
    return pl.pallas_call(
        _cf_kernel,
        grid=(nb // bb, t // tc),
        in_specs=[
            pl.BlockSpec((tc, bb, d), lambda b, i: (i, b, 0)),
            pl.BlockSpec((tc, bb, 2 * d), lambda b, i: (i, b, 0)),
            pl.BlockSpec((hist, bb, d), lambda b, i: (0, b, 0)),
            pl.BlockSpec(dw.shape, const2),
            pl.BlockSpec((1, d), const2),
            pl.BlockSpec((1, d), const2),
            pl.BlockSpec((1, d), const2),
            pl.BlockSpec((d, d), const2),
            pl.BlockSpec((1, d), const2),
        ],
        out_specs=[
            pl.BlockSpec((tc, bb, d), lambda b, i: (i, b, 0)),
            pl.BlockSpec((hist, bb, d), lambda b, i: (0, b, 0)),
        ],
        out_shape=[
            jax.ShapeDtypeStruct((t, nb, d), F32),
            jax.ShapeDtypeStruct((hist, nb, d), F32),
        ],
        scratch_shapes=[
            pltpu.VMEM((tc + hist, bb, d), F32),
            pltpu.VMEM((tc, bb, d), F32),
            pltpu.VMEM((taps, SUBLANES, d), F32),
        ],
        compiler_params=_cparams(("parallel", "arbitrary")),
        name="cf_block",
    )(x_tm, p_tm, conv0, dw, dwb, lng, lnb, w2, b2)


U32 = jnp.uint32
BF16_ROWS = 16


def _pack_rows(v_bf16):
    return pltpu.bitcast(v_bf16, U32)


def _unpack_rows(words):
    return pltpu.bitcast(words, BF16)


def _dup_bf16_words(v):
    hi = pltpu.bitcast(v.astype(BF16).astype(F32), U32)
    return hi | (hi >> 16)


def _pack_row_groups_xla(w):
    r, c = w.shape
    groups = w.reshape(r // BF16_ROWS, 2, SUBLANES, c)

    def high_half_bits(x):
        return lax.bitcast_convert_type(x.astype(BF16).astype(F32), U32)

    words = (high_half_bits(groups[:, 0]) >> 16) | high_half_bits(groups[:, 1])
    return words.reshape(r // 2, c)


def _paired_order(n):
    pos = np.arange(n)
    grp, q = pos // BF16_ROWS, pos % BF16_ROWS
    return grp * BF16_ROWS + (q % 2) * SUBLANES + q // 2


def _gelu_tanh(x):
    k1 = 2.0 * 0.7978845608028654
    k2 = k1 * 0.044715
    neg_z = x * (x * x * (-k2) + (-k1))
    return x / (1.0 + jnp.exp(neg_z))


def _merge_exchange_network(n):
    pairs = []
    p = 1
    while p < n:
        k = p
        while k >= 1:
            for j in range(k % p, n - k, 2 * k):
                for i in range(min(k, n - j - k)):
                    if (i + j) // (2 * p) == (i + j + k) // (2 * p):
                        pairs.append((i + j, i + j + k))
            k //= 2
        p *= 2
    return pairs


def _sorted_topk(s, sv_ref, want_rank):
    tops = _merged_tops([s[i * SUBLANES:(i + 1) * SUBLANES, :]
                         for i in range(s.shape[0] // SUBLANES)])
    for t, m in enumerate(tops):
        sv_ref[t:t + 1, :] = m
    if not want_rank:
        return None
    rank = jnp.full(s.shape, float(TOPK), F32)
    for t in reversed(range(TOPK)):
        rank = jnp.where(s >= tops[t], float(t), rank)
    return rank


def _merged_tops(tiles):
    v = list(tiles)
    n_tiles = len(v)
    pow2 = 1
    while pow2 < n_tiles:
        pow2 *= 2
    for a, b in _merge_exchange_network(pow2):
        if b < n_tiles:
            v[a], v[b] = jnp.maximum(v[a], v[b]), jnp.minimum(v[a], v[b])
    tops = []
    for t in range(TOPK):
        m = jnp.max(v[0], axis=0, keepdims=True)
        tops.append(m)
        if t + 1 < TOPK:
            hit = v[0] == m
            depth = min(TOPK - 1 - t, n_tiles - 1)
            for i in range(depth):
                v[i] = jnp.where(hit, v[i + 1], v[i])
            if depth == n_tiles - 1:
                v[depth] = jnp.where(hit, -jnp.inf, v[depth])
    return tops


def _peer_stats_kernel(x_ref, g_ref, wqt_ref, keys_ref, xnt_ref, rank2_ref, cnt1_ref, e1_ref,
                       e2n_ref, qt_ref, s_ref, sv1_ref, sv2_ref):
    n_heads = rank2_ref.shape[0]
    xn = _rms(x_ref[...], g_ref[...])
    xnt = xn.T.astype(BF16)
    xnt_ref[...] = _pack_rows(xnt)
    qt_ref[...] = jnp.dot(wqt_ref[...], xnt, preferred_element_type=F32)
    d_half = qt_ref.shape[0] // (2 * n_heads)

    def head_body(h, carry):
        q1 = qt_ref[pl.ds(pl.multiple_of(h * 2 * d_half, d_half), d_half), :]
        q2 = qt_ref[pl.ds(pl.multiple_of(h * 2 * d_half + d_half, d_half), d_half), :]
        s_ref[0] = jnp.dot(keys_ref[2 * h], q1, preferred_element_type=F32)
        s_ref[1] = jnp.dot(keys_ref[2 * h + 1], q2, preferred_element_type=F32)
        for c in range(s_ref.shape[2] // LANES):
            lane_tile_stats(h, slice(c * LANES, (c + 1) * LANES))
        return carry

    def lane_tile_stats(h, cols):
        s1 = s_ref[0, :, cols]
        s2 = s_ref[1, :, cols]
        _sorted_topk(s1, sv1_ref, False)
        rank2 = _sorted_topk(s2, sv2_ref, True)
        sv1 = sv1_ref[...]
        sv2 = sv2_ref[...]
        half = TOPK // 2
        cand = [sv1[0:1] + sv2[0:half], sv1[0:1] + sv2[half:]]
        cand += [sv1[i:i + 1] + sv2[0:half] for i in range(1, half)]
        cand += [sv1[half:] + sv2[0:1]]
        tau = _merged_tops(cand)[TOPK - 1]
        e2v = jnp.exp(sv2 - sv2[0:1])
        cnt_rows = jnp.zeros(sv1.shape, F32)
        mass = jnp.zeros(sv1.shape, F32)
        prefix = jnp.zeros_like(tau)
        for j in range(TOPK):
            prefix = prefix + e2v[j:j + 1]
            cond = (sv1 + sv2[j:j + 1]) >= tau
            cnt_rows = cnt_rows + jnp.where(cond, 1.0, 0.0)
            mass = jnp.where(cond, prefix, mass)
        z = jnp.sum(jnp.exp(sv1 - sv1[0:1]) * mass, axis=0, keepdims=True)
        cnt1 = jnp.zeros(s1.shape, F32)
        for i in range(TOPK):
            cnt1 = jnp.where(s1 == sv1[i:i + 1], cnt_rows[i:i + 1], cnt1)
        rank2_ref[h, :, cols] = _pack_rows(rank2.astype(BF16))
        cnt1_ref[h, :, cols] = _dup_bf16_words(cnt1)
        e1_ref[h, :, cols] = _dup_bf16_words(jnp.exp(s1 - sv1[0:1]))
        e2n_ref[h, :, cols] = _pack_rows((jnp.exp(s2 - sv2[0:1]) / z).astype(BF16))

    lax.fori_loop(0, n_heads, head_body, 0)


def _peer_stats(x, g, wqt, keys):
    n, d = x.shape
    tt = min(PEER_TOK_TILE, n)
    n_heads = keys.shape[0] // 2
    n_keys = keys.shape[1]
    row_spec = pl.BlockSpec((n_heads, n_keys, tt), lambda i: (0, 0, i))
    packed_spec = pl.BlockSpec((n_heads, n_keys // 2, tt), lambda i: (0, 0, i))
    row_words = jax.ShapeDtypeStruct((n_heads, n_keys, n), U32)
    packed_words = jax.ShapeDtypeStruct((n_heads, n_keys // 2, n), U32)
    return pl.pallas_call(
        _peer_stats_kernel,
        grid=(n // tt,),
        in_specs=[
            pl.BlockSpec((tt, d), lambda i: (i, 0)),
            pl.BlockSpec((1, d), lambda i: (0, 0)),
            pl.BlockSpec(wqt.shape, lambda i: (0, 0)),
            pl.BlockSpec(keys.shape, lambda i: (0, 0, 0)),
        ],
        out_specs=[pl.BlockSpec((d // 2, tt), lambda i: (0, i)), packed_spec, row_spec, row_spec,
                   packed_spec],
        out_shape=[jax.ShapeDtypeStruct((d // 2, n), U32), packed_words, row_words, row_words,
                   packed_words],
        scratch_shapes=[
            pltpu.VMEM((wqt.shape[0], tt), F32),
            pltpu.VMEM((2, n_keys, tt), F32),
            pltpu.VMEM((TOPK, LANES), F32),
            pltpu.VMEM((TOPK, LANES), F32),
        ],
        compiler_params=_cparams(("parallel",)),
        name="peer_stats",
    )(x, g, wqt, keys)


K1_BLOCK = 8


def _peer_dense_kernel(x_ref, xnt_ref, rank2_ref, cnt1_ref, e1_ref, e2n_ref, u_ref, v_ref,
                       gf_ref, o_ref, acc_ref, a_ref, w_ref, *, n_j, final_norm):
    n_heads, n_keys, tt = cnt1_ref.shape[0], 2 * rank2_ref.shape[1], rank2_ref.shape[2]
    et = 2 * u_ref.shape[0]
    n_k1 = et // n_keys
    j = pl.program_id(1)
    packed = (n_keys // BF16_ROWS, BF16_ROWS, LANES)

    d = v_ref.shape[1]
    a_slices = [slice(s * MXU_DIM, (s + 1) * MXU_DIM) for s in range(tt // MXU_DIM)]
    v_slices = [slice(s * MXU_DIM, (s + 1) * MXU_DIM) for s in range(d // MXU_DIM)]

    def activation_matmul(slot, cols):
        a_ref[slot, :, cols] = jnp.dot(_unpack_rows(u_ref[...]), _unpack_rows(xnt_ref[:, cols]),
                                       preferred_element_type=F32)

    def value_matmul(slot, cols):
        return lax.dot_general(w_ref[slot], _unpack_rows(v_ref[:, cols]),
                               (((0,), (0,)), ((), ())), preferred_element_type=F32)

    def row_tile(ref, h, kk, cols):
        words = jnp.broadcast_to(ref[h, kk:kk + 1, cols], (SUBLANES, LANES))
        return _unpack_rows(words)[None]

    units = [(kb, c) for kb in range(0, n_k1, K1_BLOCK) for c in range(tt // LANES)]

    def build_weighted(slot, unit_ids):
        for kb, c in [units[n] for n in unit_ids]:
            cols = slice(c * LANES, (c + 1) * LANES)
            g = [None] * K1_BLOCK
            for h in range(n_heads):
                r2 = _unpack_rows(rank2_ref[h, :, cols]).reshape(packed)
                e2 = _unpack_rows(e2n_ref[h, :, cols]).reshape(packed)
                for q in range(K1_BLOCK):
                    kk = kb + q
                    sel = jnp.where(r2 < row_tile(cnt1_ref, h, kk, cols), e2, jnp.zeros_like(e2))
                    term = sel * row_tile(e1_ref, h, kk, cols)
                    g[q] = term if h == 0 else g[q] + term
            for q in range(K1_BLOCK):
                rows = slice((kb + q) * n_keys, (kb + q + 1) * n_keys)
                act = _gelu_tanh(a_ref[slot, rows, cols].astype(BF16))
                w_ref[slot, rows, cols] = g[q].reshape(n_keys, LANES) * act

    @pl.when(j == 0)
    def _():
        acc_ref[...] = jnp.zeros_like(acc_ref)
        w_ref[...] = jnp.zeros_like(w_ref)
        for cols in a_slices:
            activation_matmul(0, cols)

    n_groups = len(a_slices) + len(v_slices)
    lead = min(len(units), n_groups)
    bounds = [min(g, lead) * len(units) // lead for g in range(n_groups + 1)]
    assert bounds[-1] == len(units)

    for cur in range(2):
        @pl.when((j >= 1) & (j <= n_j) & (j % 2 == cur))
        def _():
            for g in range(n_groups):
                build_weighted(1 - cur, range(bounds[g], bounds[g + 1]))
                if g < len(a_slices):
                    activation_matmul(cur, a_slices[g])
                else:
                    cols = v_slices[g - len(a_slices)]
                    acc_ref[:, cols] += value_matmul(cur, cols)

    @pl.when(j == n_j + 1)
    def _():
        for cols in v_slices:
            y = x_ref[:, cols] + acc_ref[:, cols] + value_matmul((n_j + 1) % 2, cols)
            if final_norm:
                acc_ref[:, cols] = y
            else:
                o_ref[:, cols] = y
        if final_norm:
            o_ref[...] = _rms(acc_ref[...], gf_ref[...])


def _peer_dense(x, xnt, rank2, cnt1, e1, e2n, u_pk, v_pk, g_final, final_norm):
    n, d = x.shape
    tt = min(PEER_TOK_TILE, n)
    n_heads, n_keys, _ = cnt1.shape
    et = PEER_EXP_TILE
    n_j = 2 * u_pk.shape[0] // et
    n_k1 = et // n_keys
    last = n_j - 1
    packed_spec = pl.BlockSpec((n_heads, n_keys // 2, tt), lambda i, j: (0, 0, i))
    k1_spec = pl.BlockSpec((n_heads, n_k1, tt),
                           lambda i, j: (0, jnp.clip(j - 1, 0, last), i))
    return pl.pallas_call(
        functools.partial(_peer_dense_kernel, n_j=n_j, final_norm=final_norm),
        grid=(n // tt, n_j + 2),
        in_specs=[
            pl.BlockSpec((tt, d), lambda i, j: (i, 0)),
            pl.BlockSpec((d // 2, tt), lambda i, j: (0, i)),
            packed_spec, k1_spec, k1_spec, packed_spec,
            pl.BlockSpec((et // 2, d), lambda i, j: (jnp.minimum(j, last), 0)),
            pl.BlockSpec((et // 2, d), lambda i, j: (jnp.clip(j - 2, 0, last), 0)),
            pl.BlockSpec((1, d), lambda i, j: (0, 0)),
        ],
        out_specs=pl.BlockSpec((tt, d), lambda i, j: (i, 0)),
        out_shape=jax.ShapeDtypeStruct((n, d), F32),
        scratch_shapes=[
            pltpu.VMEM((tt, d), F32),
            pltpu.VMEM((2, et, tt), F32),
            pltpu.VMEM((2, et, tt), BF16),
        ],
        compiler_params=_cparams(("parallel", "arbitrary")),
        name="peer_dense",
    )(x, xnt, rank2, cnt1, e1, e2n, u_pk, v_pk, g_final)


def _peer(x, g, wqt, keys, u_pk, v_pk, g_final, final_norm):
    xnt, rank2, cnt1, e1, e2n = _peer_stats(x, g, wqt, keys)
    return _peer_dense(x, xnt, rank2, cnt1, e1, e2n, u_pk, v_pk, g_final, final_norm)


def _block_diag_groups(w):
    n_blocks, bs, _ = w.shape
    per = MXU_DIM // bs
    w4 = w.reshape(n_blocks // per, per, bs, bs)
    eye = jnp.eye(per, dtype=w.dtype)
    bd = jnp.einsum('gaij,ab->gaibj', w4, eye)
    return bd.reshape(n_blocks // per, MXU_DIM, MXU_DIM).astype(BF16)


def _prep_weights(p):
    row = lambda v: v.reshape(1, -1)
    depth = p['norm_mix'].shape[0]
    w = dict(p)
    w['rg_w_in'] = p['rg_w_in'][0].astype(BF16)
    w['rg_wa_bd'] = _block_diag_groups(p['rg_w_a'][0])
    w['rg_wi_bd'] = _block_diag_groups(p['rg_w_i'][0])
    w['rg_w_out'] = p['rg_w_out'][0].astype(BF16)
    w['cf_w_pw1'] = p['cf_w_pw1'][0].astype(BF16)
    w['cf_w_pw2'] = p['cf_w_pw2'][0].astype(BF16)
    w['peer_wqt'] = [p['peer_w_q'][i].T.astype(BF16) for i in range(depth)]
    n_keys = p['peer_sub_keys'].shape[-2]
    keys = p['peer_sub_keys'].at[:, :, 1].set(p['peer_sub_keys'][:, :, 1][:, :, _paired_order(n_keys)])
    w['peer_keys'] = [keys[i].reshape((-1,) + keys.shape[-2:]) for i in range(depth)]
    w['peer_u_pk'] = [_pack_row_groups_xla(p['peer_u'][i]) for i in range(depth)]
    w['peer_v_pk'] = [_pack_row_groups_xla(p['peer_v'][i]) for i in range(depth)]
    w['row'] = row
    return w


def _trunk(x_btd, rg_h0, rg_conv0, cf_conv0, w):
    nb, t, d = x_btd.shape
    row = w['row']
    x = jnp.transpose(x_btd, (1, 0, 2))
    n = t * nb
    zeros2d = jnp.zeros((1, 2 * d), F32)

    proj = _norm_matmul(x.reshape(n, d), row(w['norm_mix'][0]), w['rg_w_in'], zeros2d)
    x, h_last, rg_cs = _rg_block(
        x, proj.reshape(t, nb, 2 * d), rg_h0[0], jnp.transpose(rg_conv0[0], (1, 0, 2)),
        w['rg_conv_w'][0], row(w['rg_conv_b'][0]), w['rg_wa_bd'], row(w['rg_b_a'][0]),
        w['rg_wi_bd'], row(w['rg_b_i'][0]), row(w['rg_lambda'][0]), w['rg_w_out'])
    x = _peer(x.reshape(n, d), row(w['norm_ffn'][0]), w['peer_wqt'][0], w['peer_keys'][0],
              w['peer_u_pk'][0], w['peer_v_pk'][0], row(w['norm_final']), False)

    pp = _norm_matmul(x, row(w['norm_mix'][1]), w['cf_w_pw1'], row(w['cf_b_pw1'][0]))
    x, cf_cs = _cf_block(
        x.reshape(t, nb, d), pp.reshape(t, nb, 2 * d), jnp.transpose(cf_conv0[0], (1, 0, 2)),
        w['cf_dw_w'][0], row(w['cf_dw_b'][0]), row(w['cf_ln_g'][0]), row(w['cf_ln_b'][0]),
        w['cf_w_pw2'], row(w['cf_b_pw2'][0]))
    y = _peer(x.reshape(n, d), row(w['norm_ffn'][1]), w['peer_wqt'][1], w['peer_keys'][1],
              w['peer_u_pk'][1], w['peer_v_pk'][1], row(w['norm_final']), True)

    y = jnp.transpose(y.reshape(t, nb, d), (1, 0, 2))
    return (y, h_last[None], jnp.transpose(rg_cs, (1, 0, 2))[None],
            jnp.transpose(cf_cs, (1, 0, 2))[None])


def kernel(x_prompt, x_sample, state_rglru_h, state_rglru_conv, state_conformer_conv, norm_mix, norm_ffn, norm_final, rg_w_in, rg_conv_w, rg_conv_b, rg_w_a, rg_b_a, rg_w_i, rg_b_i, rg_lambda, rg_w_out, cf_w_pw1, cf_b_pw1, cf_dw_w, cf_dw_b, cf_ln_g, cf_ln_b, cf_w_pw2, cf_b_pw2, peer_w_q, peer_sub_keys, peer_u, peer_v):
    assert norm_mix.shape[0] == 2 and rg_w_in.shape[0] == 1 and cf_w_pw1.shape[0] == 1
    params = dict(norm_mix=norm_mix, norm_ffn=norm_ffn, norm_final=norm_final,
                  rg_w_in=rg_w_in, rg_conv_w=rg_conv_w, rg_conv_b=rg_conv_b,
                  rg_w_a=rg_w_a, rg_b_a=rg_b_a, rg_w_i=rg_w_i, rg_b_i=rg_b_i,
                  rg_lambda=rg_lambda, rg_w_out=rg_w_out,
                  cf_w_pw1=cf_w_pw1, cf_b_pw1=cf_b_pw1, cf_dw_w=cf_dw_w, cf_dw_b=cf_dw_b,
                  cf_ln_g=cf_ln_g, cf_ln_b=cf_ln_b, cf_w_pw2=cf_w_pw2, cf_b_pw2=cf_b_pw2,
                  peer_w_q=peer_w_q, peer_sub_keys=peer_sub_keys, peer_u=peer_u, peer_v=peer_v)
    w = _prep_weights(params)
    dt = x_prompt.dtype
    nbp, _, d = x_prompt.shape
    n_a, n_b = state_rglru_h.shape[0], state_conformer_conv.shape[0]
    h0_p = jnp.zeros((n_a, nbp, d), dt)
    rgc0_p = jnp.zeros((n_a, nbp) + state_rglru_conv.shape[2:], dt)
    cfc0_p = jnp.zeros((n_b, nbp) + state_conformer_conv.shape[2:], dt)
    y_p, p_h, p_rgc, p_cfc = _trunk(x_prompt, h0_p, rgc0_p, cfc0_p, w)
    y_s, s_h, s_rgc, s_cfc = _trunk(x_sample, state_rglru_h, state_rglru_conv,
                                    state_conformer_conv, w)
    return (y_p, y_s, p_h, p_rgc, p_cfc, s_h, s_rgc, s_cfc)
```

```python
import functools

import jax
import jax.numpy as jnp
import numpy as np
from jax import lax
from jax.experimental import pallas as pl
from jax.experimental.pallas import tpu as pltpu

F32 = jnp.float32
BF16 = jnp.bfloat16

EPS = 1e-6
RG_C = 8.0
TOPK = 16

LANES = 128
SUBLANES = 8
MXU_DIM = 256
VMEM_LIMIT = 56 * 1024 * 1024

ROW_TILE = 512
PEER_TOK_TILE = 512
PEER_EXP_TILE = 1024


def _rms(x, g):
    return x * lax.rsqrt(jnp.mean(x * x, axis=-1, keepdims=True) + EPS) * g


def _cparams(sem):
    return pltpu.CompilerParams(dimension_semantics=sem, vmem_limit_bytes=VMEM_LIMIT)


def _norm_matmul_kernel(x_ref, g_ref, w_ref, b_ref, o_ref):
    xn = _rms(x_ref[...], g_ref[...])
    o_ref[...] = jnp.dot(xn.astype(BF16), w_ref[...], preferred_element_type=F32) + b_ref[...]


def _norm_matmul(x, g, w, b):
    n, d = x.shape
    n_out = w.shape[1]
    tm = min(ROW_TILE, n)
    return pl.pallas_call(
        _norm_matmul_kernel,
        grid=(n // tm,),
        in_specs=[
            pl.BlockSpec((tm, d), lambda i: (i, 0)),
            pl.BlockSpec((1, d), lambda i: (0, 0)),
            pl.BlockSpec((d, n_out), lambda i: (0, 0)),
            pl.BlockSpec((1, n_out), lambda i: (0, 0)),
        ],
        out_specs=pl.BlockSpec((tm, n_out), lambda i: (i, 0)),
        out_shape=jax.ShapeDtypeStruct((n, n_out), F32),
        compiler_params=_cparams(("parallel",)),
        name="norm_matmul",
    )(x, g, w, b)


def _rg_kernel(x_ref, proj_ref, h0_ref, conv0_ref, cw_ref, cb_ref, wa_ref, ba_ref, wi_ref,
               bi_ref, lam_ref, wout_ref, o_ref, hlast_ref, cstate_ref,
               ext_ref, a_ref, u_ref, hcar_ref):
    tc, nb, d = x_ref.shape
    rows = tc * nb
    cw_taps = cw_ref.shape[0]
    hist = cw_taps - 1
    step = pl.program_id(1)

    @pl.when(step == 0)
    def _():
        ext_ref[0:hist] = conv0_ref[...]
        hcar_ref[...] = h0_ref[...]

    @pl.when(step > 0)
    def _():
        ext_ref[0:hist] = ext_ref[tc:tc + hist]

    ext_ref[hist:hist + tc] = proj_ref[:, :, d:]

    xc = cb_ref[...][None]
    for k in range(cw_taps):
        xc = xc + cw_ref[k:k + 1, :][None] * ext_ref[k:k + tc]
    xc2 = xc.reshape(rows, d)
    xcb = xc2.astype(BF16)

    def blockdiag(w_ref):
        n_grp = w_ref.shape[0]
        gw = d // n_grp
        outs = [jnp.dot(xcb[:, g * gw:(g + 1) * gw], w_ref[g], preferred_element_type=F32)
                for g in range(n_grp)]
        return jnp.concatenate(outs, axis=-1)

    r = jax.nn.sigmoid(blockdiag(wa_ref) + ba_ref[...])
    gi = jax.nn.sigmoid(blockdiag(wi_ref) + bi_ref[...])
    neg_lam = -lam_ref[...]
    softplus = jnp.maximum(neg_lam, 0.0) + jnp.log1p(jnp.exp(-jnp.abs(neg_lam)))
    log_a = (-RG_C) * r * softplus
    a = jnp.exp(log_a)
    u = jnp.sqrt(-jnp.tanh(log_a) * (a * a + 1.0)) * (gi * xc2)
    a_ref[...] = a.reshape(tc, nb, d)
    u_ref[...] = u.reshape(tc, nb, d)

    def scan_body(t, h):
        h = a_ref[t] * h + u_ref[t]
        u_ref[t] = h
        return h

    h_fin = lax.fori_loop(0, tc, scan_body, hcar_ref[...], unroll=min(tc, 8))
    hcar_ref[...] = h_fin

    gate = jax.nn.gelu(proj_ref[:, :, :d].reshape(rows, d))
    gated = (u_ref[...].reshape(rows, d) * gate).astype(BF16)
    y = jnp.dot(gated, wout_ref[...], preferred_element_type=F32)
    o_ref[...] = x_ref[...] + y.reshape(tc, nb, d)

    @pl.when(step == pl.num_programs(1) - 1)
    def _():
        hlast_ref[...] = h_fin
        cstate_ref[...] = ext_ref[tc:tc + hist]


def _seq_tiles(t, nb, max_bb):
    bb = min(nb, max_bb, max(SUBLANES, ROW_TILE // t))
    tc = min(t, ROW_TILE // bb)
    assert nb % bb == 0 and t % tc == 0 and bb % SUBLANES == 0
    return bb, tc


def _rg_block(x_tm, proj_tm, h0, conv0, cw, cb, wa_bd, ba, wi_bd, bi, lam, wout):
    t, nb, d = x_tm.shape
    bb, tc = _seq_tiles(t, nb, ROW_TILE)
    hist = cw.shape[0] - 1
    assert t == tc or tc >= hist
    const2 = lambda b, i: (0, 0)
    const3 = lambda b, i: (0, 0, 0)
    return pl.pallas_call(
        _rg_kernel,
        grid=(nb // bb, t // tc),
        in_specs=[
            pl.BlockSpec((tc, bb, d), lambda b, i: (i, b, 0)),
            pl.BlockSpec((tc, bb, 2 * d), lambda b, i: (i, b, 0)),
            pl.BlockSpec((bb, d), lambda b, i: (b, 0)),
            pl.BlockSpec((hist, bb, d), lambda b, i: (0, b, 0)),
            pl.BlockSpec(cw.shape, const2),
            pl.BlockSpec((1, d), const2),
            pl.BlockSpec(wa_bd.shape, const3),
            pl.BlockSpec((1, d), const2),
            pl.BlockSpec(wi_bd.shape, const3),
            pl.BlockSpec((1, d), const2),
            pl.BlockSpec((1, d), const2),
            pl.BlockSpec((d, d), const2),
        ],
        out_specs=[
            pl.BlockSpec((tc, bb, d), lambda b, i: (i, b, 0)),
            pl.BlockSpec((bb, d), lambda b, i: (b, 0)),
            pl.BlockSpec((hist, bb, d), lambda b, i: (0, b, 0)),
        ],
        out_shape=[
            jax.ShapeDtypeStruct((t, nb, d), F32),
            jax.ShapeDtypeStruct((nb, d), F32),
            jax.ShapeDtypeStruct((hist, nb, d), F32),
        ],
        scratch_shapes=[
            pltpu.VMEM((tc + hist, bb, d), F32),
            pltpu.VMEM((tc, bb, d), F32),
            pltpu.VMEM((tc, bb, d), F32),
            pltpu.VMEM((bb, d), F32),
        ],
        compiler_params=_cparams(("parallel", "arbitrary")),
        name="rg_block",
    )(x_tm, proj_tm, h0, conv0, cw, cb, wa_bd, ba, wi_bd, bi, lam, wout)


CONV_T_SUB = 4
CF_MAX_BATCH_ROWS = 32


def _cf_kernel(x_ref, p_ref, conv0_ref, dw_ref, dwb_ref, lng_ref, lnb_ref, w2_ref, b2_ref,
               o_ref, cstate_ref, ext_ref, c_ref, wb_ref):
    tc, nb, d = x_ref.shape
    rows = tc * nb
    taps = dw_ref.shape[0]
    hist = taps - 1
    step = pl.program_id(1)

    @pl.when(step == 0)
    def _():
        ext_ref[0:hist] = conv0_ref[...]
        for k in range(taps):
            wb_ref[k] = jnp.broadcast_to(dw_ref[k:k + 1, :], (SUBLANES, d))

    @pl.when(step > 0)
    def _():
        ext_ref[0:hist] = ext_ref[tc:tc + hist]

    ext_ref[hist:hist + tc] = p_ref[:, :, :d] * jax.nn.sigmoid(p_ref[:, :, d:])

    n_b = nb // SUBLANES
    bias = jnp.broadcast_to(dwb_ref[...], (SUBLANES, d))[None]

    def conv_body(idx, carry):
        t0 = (idx // n_b) * CONV_T_SUB
        b0 = pl.multiple_of((idx % n_b) * SUBLANES, SUBLANES)
        acc = jnp.broadcast_to(bias, (CONV_T_SUB, SUBLANES, d))
        for k in range(taps):
            acc = acc + wb_ref[k][None] * ext_ref[pl.ds(t0 + k, CONV_T_SUB), pl.ds(b0, SUBLANES), :]
        c_ref[pl.ds(t0, CONV_T_SUB), pl.ds(b0, SUBLANES), :] = acc
        return carry

    lax.fori_loop(0, (tc // CONV_T_SUB) * n_b, conv_body, 0)

    c = c_ref[...].reshape(rows, d)
    mu = jnp.mean(c, axis=-1, keepdims=True)
    cc = c - mu
    var = jnp.mean(cc * cc, axis=-1, keepdims=True)
    y = cc * lax.rsqrt(var + EPS) * lng_ref[...] + lnb_ref[...]
    y = jax.nn.silu(y).astype(BF16)
    out = jnp.dot(y, w2_ref[...], preferred_element_type=F32) + b2_ref[...]
    o_ref[...] = x_ref[...] + out.reshape(tc, nb, d)

    @pl.when(step == pl.num_programs(1) - 1)
    def _():
        cstate_ref[...] = ext_ref[tc:tc + hist]


def _cf_block(x_tm, p_tm, conv0, dw, dwb, lng, lnb, w2, b2):
    t, nb, d = x_tm.shape
    bb, tc = _seq_tiles(t, nb, CF_MAX_BATCH_ROWS)
    taps = dw.shape[0]
    hist = taps - 1
    assert tc % CONV_T_SUB == 0
    assert t == tc or tc >= hist
    const2 = lambda b, i: (0, 0)
    return pl.pallas_call(
        _cf_kernel,
        grid=(nb // bb, t // tc),
        in_specs=[
            pl.BlockSpec((tc, bb, d), lambda b, i: (i, b, 0)),
            pl.BlockSpec((tc, bb, 2 * d), lambda b, i: (i, b, 0)),
            pl.BlockSpec((hist, bb, d), lambda b, i: (0, b, 0)),
            pl.BlockSpec(dw.shape, const2),
            pl.BlockSpec((1, d), const2),
            pl.BlockSpec((1, d), const2),
            pl.BlockSpec((1, d), const2),
            pl.BlockSpec((d, d), const2),
            pl.BlockSpec((1, d), const2),
        ],
        out_specs=[
            pl.BlockSpec((tc, bb, d), lambda b, i: (i, b, 0)),
            pl.BlockSpec((hist, bb, d), lambda b, i: (0, b, 0)),
        ],
        out_shape=[
            jax.ShapeDtypeStruct((t, nb, d), F32),
            jax.ShapeDtypeStruct((hist, nb, d), F32),
        ],
        scratch_shapes=[
            pltpu.VMEM((tc + hist, bb, d), F32),
            pltpu.VMEM((tc, bb, d), F32),
            pltpu.VMEM((taps, SUBLANES, d), F32),
        ],
        compiler_params=_cparams(("parallel", "arbitrary")),
        name="cf_block",
    )(x_tm, p_tm, conv0, dw, dwb, lng, lnb, w2, b2)


U32 = jnp.uint32
BF16_ROWS = 16


def _pack_rows(v_bf16):
    return pltpu.bitcast(v_bf16, U32)


def _unpack_rows(words):
    return pltpu.bitcast(words, BF16)


def _dup_bf16_words(v):
    hi = pltpu.bitcast(v.astype(BF16).astype(F32), U32)
    return hi | (hi >> 16)


def _pack_row_groups_xla(w):
    r, c = w.shape
    groups = w.reshape(r // BF16_ROWS, 2, SUBLANES, c)

    def high_half_bits(x):
        return lax.bitcast_convert_type(x.astype(BF16).astype(F32), U32)

    words = (high_half_bits(groups[:, 0]) >> 16) | high_half_bits(groups[:, 1])
    return words.reshape(r // 2, c)


def _paired_order(n):
    pos = np.arange(n)
    grp, q = pos // BF16_ROWS, pos % BF16_ROWS
    return grp * BF16_ROWS + (q % 2) * SUBLANES + q // 2


def _gelu_tanh(x):
    k1 = 2.0 * 0.7978845608028654
    k2 = k1 * 0.044715
    neg_z = x * (x * x * (-k2) + (-k1))
    return x / (1.0 + jnp.exp(neg_z))


def _merge_exchange_network(n):
    pairs = []
    p = 1
    while p < n:
        k = p
        while k >= 1:
            for j in range(k % p, n - k, 2 * k):
                for i in range(min(k, n - j - k)):
                    if (i + j) // (2 * p) == (i + j + k) // (2 * p):
                        pairs.append((i + j, i + j + k))
            k //= 2
        p *= 2
    return pairs


def _sorted_topk(s, sv_ref, want_rank):
    tops = _merged_tops([s[i * SUBLANES:(i + 1) * SUBLANES, :]
                         for i in range(s.shape[0] // SUBLANES)])
    for t, m in enumerate(tops):
        sv_ref[t:t + 1, :] = m
    if not want_rank:
        return None
    rank = jnp.full(s.shape, float(TOPK), F32)
    for t in reversed(range(TOPK)):
        rank = jnp.where(s >= tops[t], float(t), rank)
    return rank


def _merged_tops(tiles):
    v = list(tiles)
    n_tiles = len(v)
    pow2 = 1
    while pow2 < n_tiles:
        pow2 *= 2
    for a, b in _merge_exchange_network(pow2):
        if b < n_tiles:
            v[a], v[b] = jnp.maximum(v[a], v[b]), jnp.minimum(v[a], v[b])
    tops = []
    for t in range(TOPK):
        m = jnp.max(v[0], axis=0, keepdims=True)
        tops.append(m)
        if t + 1 < TOPK:
            hit = v[0] == m
            depth = min(TOPK - 1 - t, n_tiles - 1)
            for i in range(depth):
                v[i] = jnp.where(hit, v[i + 1], v[i])
            if depth == n_tiles - 1:
                v[depth] = jnp.where(hit, -jnp.inf, v[depth])
    return tops


def _peer_stats_kernel(x_ref, g_ref, wqt_ref, keys_ref, xnt_ref, rank2_ref, cnt1_ref, e1_ref,
                       e2n_ref, qt_ref, s_ref, sv1_ref, sv2_ref):
    n_heads = rank2_ref.shape[0]
    xn = _rms(x_ref[...], g_ref[...])
    xnt = xn.T.astype(BF16)
    xnt_ref[...] = _pack_rows(xnt)
    qt_ref[...] = jnp.dot(wqt_ref[...], xnt, preferred_element_type=F32)
    d_half = qt_ref.shape[0] // (2 * n_heads)

    def head_body(h, carry):
        q1 = qt_ref[pl.ds(pl.multiple_of(h * 2 * d_half, d_half), d_half), :]
        q2 = qt_ref[pl.ds(pl.multiple_of(h * 2 * d_half + d_half, d_half), d_half), :]
        s_ref[0] = jnp.dot(keys_ref[2 * h], q1, preferred_element_type=F32)
        s_ref[1] = jnp.dot(keys_ref[2 * h + 1], q2, preferred_element_type=F32)
        for c in range(s_ref.shape[2] // LANES):
            lane_tile_stats(h, slice(c * LANES, (c + 1) * LANES))
        return carry

    def lane_tile_stats(h, cols):
        s1 = s_ref[0, :, cols]
        s2 = s_ref[1, :, cols]
        _sorted_topk(s1, sv1_ref, False)
        rank2 = _sorted_topk(s2, sv2_ref, True)
        sv1 = sv1_ref[...]
        sv2 = sv2_ref[...]
        half = TOPK // 2
        cand = [sv1[0:1] + sv2[0:half], sv1[0:1] + sv2[half:]]
        cand += [sv1[i:i + 1] + sv2[0:half] for i in range(1, half)]
        cand += [sv1[half:] + sv2[0:1]]
        tau = _merged_tops(cand)[TOPK - 1]
        e2v = jnp.exp(sv2 - sv2[0:1])
        cnt_rows = jnp.zeros(sv1.shape, F32)
        mass = jnp.zeros(sv1.shape, F32)
        prefix = jnp.zeros_like(tau)
        for j in range(TOPK):
            prefix = prefix + e2v[j:j + 1]
            cond = (sv1 + sv2[j:j + 1]) >= tau
            cnt_rows = cnt_rows + jnp.where(cond, 1.0, 0.0)
            mass = jnp.where(cond, prefix, mass)
        z = jnp.sum(jnp.exp(sv1 - sv1[0:1]) * mass, axis=0, keepdims=True)
        cnt1 = jnp.zeros(s1.shape, F32)
        for i in range(TOPK):
            cnt1 = jnp.where(s1 == sv1[i:i + 1], cnt_rows[i:i + 1], cnt1)
        rank2_ref[h, :, cols] = _pack_rows(rank2.astype(BF16))
        cnt1_ref[h, :, cols] = _dup_bf16_words(cnt1)
        e1_ref[h, :, cols] = _dup_bf16_words(jnp.exp(s1 - sv1[0:1]))
        e2n_ref[h, :, cols] = _pack_rows((jnp.exp(s2 - sv2[0:1]) / z).astype(BF16))

    lax.fori_loop(0, n_heads, head_body, 0, unroll=2)


def _peer_stats(x, g, wqt, keys):
    n, d = x.shape
    tt = min(PEER_TOK_TILE, n)
    n_heads = keys.shape[0] // 2
    n_keys = keys.shape[1]
    row_spec = pl.BlockSpec((n_heads, n_keys, tt), lambda i: (0, 0, i))
    packed_spec = pl.BlockSpec((n_heads, n_keys // 2, tt), lambda i: (0, 0, i))
    row_words = jax.ShapeDtypeStruct((n_heads, n_keys, n), U32)
    packed_words = jax.ShapeDtypeStruct((n_heads, n_keys // 2, n), U32)
    return pl.pallas_call(
        _peer_stats_kernel,
        grid=(n // tt,),
        in_specs=[
            pl.BlockSpec((tt, d), lambda i: (i, 0)),
            pl.BlockSpec((1, d), lambda i: (0, 0)),
            pl.BlockSpec(wqt.shape, lambda i: (0, 0)),
            pl.BlockSpec(keys.shape, lambda i: (0, 0, 0)),
        ],
        out_specs=[pl.BlockSpec((d // 2, tt), lambda i: (0, i)), packed_spec, row_spec, row_spec,
                   packed_spec],
        out_shape=[jax.ShapeDtypeStruct((d // 2, n), U32), packed_words, row_words, row_words,
                   packed_words],
        scratch_shapes=[
            pltpu.VMEM((wqt.shape[0], tt), F32),
            pltpu.VMEM((2, n_keys, tt), F32),
            pltpu.VMEM((TOPK, LANES), F32),
            pltpu.VMEM((TOPK, LANES), F32),
        ],
        compiler_params=_cparams(("parallel",)),
        name="peer_stats",
    )(x, g, wqt, keys)


K1_BLOCK = 8


def _peer_dense_kernel(x_ref, xnt_ref, rank2_ref, cnt1_ref, e1_ref, e2n_ref, u_ref, v_ref,
                       gf_ref, o_ref, acc_ref, a_ref, w_ref, *, n_j, final_norm):
    n_heads, n_keys, tt = cnt1_ref.shape[0], 2 * rank2_ref.shape[1], rank2_ref.shape[2]
    et = 2 * u_ref.shape[0]
    n_k1 = et // n_keys
    j = pl.program_id(1)
    packed = (n_keys // BF16_ROWS, BF16_ROWS, LANES)

    d = v_ref.shape[1]
    a_slices = [slice(s * MXU_DIM, (s + 1) * MXU_DIM) for s in range(tt // MXU_DIM)]
    v_slices = [slice(s * MXU_DIM, (s + 1) * MXU_DIM) for s in range(d // MXU_DIM)]

    def activation_matmul(slot, cols):
        a_ref[slot, :, cols] = jnp.dot(_unpack_rows(u_ref[...]), _unpack_rows(xnt_ref[:, cols]),
                                       preferred_element_type=F32)

    def value_matmul(slot, cols):
        return lax.dot_general(w_ref[slot], _unpack_rows(v_ref[:, cols]),
                               (((0,), (0,)), ((), ())), preferred_element_type=F32)

    def row_tile(ref, h, kk, cols):
        words = jnp.broadcast_to(ref[h, kk:kk + 1, cols], (SUBLANES, LANES))
        return _unpack_rows(words)[None]

    units = [(kb, c) for kb in range(0, n_k1, K1_BLOCK) for c in range(tt // LANES)]

    def build_weighted(slot, unit_ids):
        for kb, c in [units[n] for n in unit_ids]:
            cols = slice(c * LANES, (c + 1) * LANES)
            g = [None] * K1_BLOCK
            for h in range(n_heads):
                r2 = _unpack_rows(rank2_ref[h, :, cols]).reshape(packed)
                e2 = _unpack_rows(e2n_ref[h, :, cols]).reshape(packed)
                for q in range(K1_BLOCK):
                    kk = kb + q
                    sel = jnp.where(r2 < row_tile(cnt1_ref, h, kk, cols), e2, jnp.zeros_like(e2))
                    term = sel * row_tile(e1_ref, h, kk, cols)
                    g[q] = term if h == 0 else g[q] + term
            for q in range(K1_BLOCK):
                rows = slice((kb + q) * n_keys, (kb + q + 1) * n_keys)
                act = _gelu_tanh(a_ref[slot, rows, cols].astype(BF16))
                w_ref[slot, rows, cols] = g[q].reshape(n_keys, LANES) * act

    @pl.when(j == 0)
    def _():
        acc_ref[...] = jnp.zeros_like(acc_ref)
        w_ref[...] = jnp.zeros_like(w_ref)
        for cols in a_slices:
            activation_matmul(0, cols)

    n_groups = len(a_slices) + len(v_slices)
    lead = min(len(units), n_groups)
    bounds = [min(g, lead) * len(units) // lead for g in range(n_groups + 1)]
    assert bounds[-1] == len(units)

    for cur in range(2):
        @pl.when((j >= 1) & (j <= n_j) & (j % 2 == cur))
        def _():
            for g in range(n_groups):
                build_weighted(1 - cur, range(bounds[g], bounds[g + 1]))
                if g < len(a_slices):
                    activation_matmul(cur, a_slices[g])
                else:
                    cols = v_slices[g - len(a_slices)]
                    acc_ref[:, cols] += value_matmul(cur, cols)

    @pl.when(j == n_j + 1)
    def _():
        for cols in v_slices:
            y = x_ref[:, cols] + acc_ref[:, cols] + value_matmul((n_j + 1) % 2, cols)
            if final_norm:
                acc_ref[:, cols] = y
            else:
                o_ref[:, cols] = y
        if final_norm:
            o_ref[...] = _rms(acc_ref[...], gf_ref[...])


def _peer_dense(x, xnt, rank2, cnt1, e1, e2n, u_pk, v_pk, g_final, final_norm):
    n, d = x.shape
    tt = min(PEER_TOK_TILE, n)
    n_heads, n_keys, _ = cnt1.shape
    et = PEER_EXP_TILE
    n_j = 2 * u_pk.shape[0] // et
    n_k1 = et // n_keys
    last = n_j - 1
    packed_spec = pl.BlockSpec((n_heads, n_keys // 2, tt), lambda i, j: (0, 0, i))
    k1_spec = pl.BlockSpec((n_heads, n_k1, tt),
                           lambda i, j: (0, jnp.clip(j - 1, 0, last), i))
    return pl.pallas_call(
        functools.partial(_peer_dense_kernel, n_j=n_j, final_norm=final_norm),
        grid=(n // tt, n_j + 2),
        in_specs=[
            pl.BlockSpec((tt, d), lambda i, j: (i, 0)),
            pl.BlockSpec((d // 2, tt), lambda i, j: (0, i)),
            packed_spec, k1_spec, k1_spec, packed_spec,
            pl.BlockSpec((et // 2, d), lambda i, j: (jnp.minimum(j, last), 0)),
            pl.BlockSpec((et // 2, d), lambda i, j: (jnp.clip(j - 2, 0, last), 0)),
            pl.BlockSpec((1, d), lambda i, j: (0, 0)),
        ],
        out_specs=pl.BlockSpec((tt, d), lambda i, j: (i, 0)),
        out_shape=jax.ShapeDtypeStruct((n, d), F32),
        scratch_shapes=[
            pltpu.VMEM((tt, d), F32),
            pltpu.VMEM((2, et, tt), F32),
            pltpu.VMEM((2, et, tt), BF16),
        ],
        compiler_params=_cparams(("parallel", "arbitrary")),
        name="peer_dense",
    )(x, xnt, rank2, cnt1, e1, e2n, u_pk, v_pk, g_final)


def _peer(x, g, wqt, keys, u_pk, v_pk, g_final, final_norm):
    xnt, rank2, cnt1, e1, e2n = _peer_stats(x, g, wqt, keys)
    return _peer_dense(x, xnt, rank2, cnt1, e1, e2n, u_pk, v_pk, g_final, final_norm)


def _block_diag_groups(w):
    n_blocks, bs, _ = w.shape
    per = MXU_DIM // bs
    w4 = w.reshape(n_blocks // per, per, bs, bs)
    eye = jnp.eye(per, dtype=w.dtype)
    bd = jnp.einsum('gaij,ab->gaibj', w4, eye)
    return bd.reshape(n_blocks // per, MXU_DIM, MXU_DIM).astype(BF16)


def _prep_weights(p):
    row = lambda v: v.reshape(1, -1)
    depth = p['norm_mix'].shape[0]
    w = dict(p)
    w['rg_w_in'] = p['rg_w_in'][0].astype(BF16)
    w['rg_wa_bd'] = _block_diag_groups(p['rg_w_a'][0])
    w['rg_wi_bd'] = _block_diag_groups(p['rg_w_i'][0])
    w['rg_w_out'] = p['rg_w_out'][0].astype(BF16)
    w['cf_w_pw1'] = p['cf_w_pw1'][0].astype(BF16)
    w['cf_w_pw2'] = p['cf_w_pw2'][0].astype(BF16)
    w['peer_wqt'] = [p['peer_w_q'][i].T.astype(BF16) for i in range(depth)]
    n_keys = p['peer_sub_keys'].shape[-2]
    keys = p['peer_sub_keys'].at[:, :, 1].set(p['peer_sub_keys'][:, :, 1][:, :, _paired_order(n_keys)])
    w['peer_keys'] = [keys[i].reshape((-1,) + keys.shape[-2:]) for i in range(depth)]
    w['peer_u_pk'] = [_pack_row_groups_xla(p['peer_u'][i]) for i in range(depth)]
    w['peer_v_pk'] = [_pack_row_groups_xla(p['peer_v'][i]) for i in range(depth)]
    w['row'] = row
    return w


def _trunk(x_btd, rg_h0, rg_conv0, cf_conv0, w):
    nb, t, d = x_btd.shape
    row = w['row']
    x = jnp.transpose(x_btd, (1, 0, 2))
    n = t * nb
    zeros2d = jnp.zeros((1, 2 * d), F32)

    proj = _norm_matmul(x.reshape(n, d), row(w['norm_mix'][0]), w['rg_w_in'], zeros2d)
    x, h_last, rg_cs = _rg_block(
        x, proj.reshape(t, nb, 2 * d), rg_h0[0], jnp.transpose(rg_conv0[0], (1, 0, 2)),
        w['rg_conv_w'][0], row(w['rg_conv_b'][0]), w['rg_wa_bd'], row(w['rg_b_a'][0]),
        w['rg_wi_bd'], row(w['rg_b_i'][0]), row(w['rg_lambda'][0]), w['rg_w_out'])
    x = _peer(x.reshape(n, d), row(w['norm_ffn'][0]), w['peer_wqt'][0], w['peer_keys'][0],
              w['peer_u_pk'][0], w['peer_v_pk'][0], row(w['norm_final']), False)

    pp = _norm_matmul(x, row(w['norm_mix'][1]), w['cf_w_pw1'], row(w['cf_b_pw1'][0]))
    x, cf_cs = _cf_block(
        x.reshape(t, nb, d), pp.reshape(t, nb, 2 * d), jnp.transpose(cf_conv0[0], (1, 0, 2)),
        w['cf_dw_w'][0], row(w['cf_dw_b'][0]), row(w['cf_ln_g'][0]), row(w['cf_ln_b'][0]),
        w['cf_w_pw2'], row(w['cf_b_pw2'][0]))
    y = _peer(x.reshape(n, d), row(w['norm_ffn'][1]), w['peer_wqt'][1], w['peer_keys'][1],
              w['peer_u_pk'][1], w['peer_v_pk'][1], row(w['norm_final']), True)

    y = jnp.transpose(y.reshape(t, nb, d), (1, 0, 2))
    return (y, h_last[None], jnp.transpose(rg_cs, (1, 0, 2))[None],
            jnp.transpose(cf_cs, (1, 0, 2))[None])


def kernel(x_prompt, x_sample, state_rglru_h, state_rglru_conv, state_conformer_conv, norm_mix, norm_ffn, norm_final, rg_w_in, rg_conv_w, rg_conv_b, rg_w_a, rg_b_a, rg_w_i, rg_b_i, rg_lambda, rg_w_out, cf_w_pw1, cf_b_pw1, cf_dw_w, cf_dw_b, cf_ln_g, cf_ln_b, cf_w_pw2, cf_b_pw2, peer_w_q, peer_sub_keys, peer_u, peer_v):
    assert norm_mix.shape[0] == 2 and rg_w_in.shape[0] == 1 and cf_w_pw1.shape[0] == 1
    params = dict(norm_mix=norm_mix, norm_ffn=norm_ffn, norm_final=norm_final,
                  rg_w_in=rg_w_in, rg_conv_w=rg_conv_w, rg_conv_b=rg_conv_b,
                  rg_w_a=rg_w_a, rg_b_a=rg_b_a, rg_w_i=rg_w_i, rg_b_i=rg_b_i,
                  rg_lambda=rg_lambda, rg_w_out=rg_w_out,
                  cf_w_pw1=cf_w_pw1, cf_b_pw1=cf_b_pw1, cf_dw_w=cf_dw_w, cf_dw_b=cf_dw_b,
                  cf_ln_g=cf_ln_g, cf_ln_b=cf_ln_b, cf_w_pw2=cf_w_pw2, cf_b_pw2=cf_b_pw2,
                  peer_w_q=peer_w_q, peer_sub_keys=peer_sub_keys, peer_u=peer_u, peer_v=peer_v)
    w = _prep_weights(params)
    dt = x_prompt.dtype
    nbp, _, d = x_prompt.shape
    n_a, n_b = state_rglru_h.shape[0], state_conformer_conv.shape[0]
    h0_p = jnp.zeros((n_a, nbp, d), dt)
    rgc0_p = jnp.zeros((n_a, nbp) + state_rglru_conv.shape[2:], dt)
    cfc0_p = jnp.zeros((n_b, nbp) + state_conformer_conv.shape[2:], dt)
    y_p, p_h, p_rgc, p_cfc = _trunk(x_prompt, h0_p, rgc0_p, cfc0_p, w)
    y_s, s_h, s_rgc, s_cfc = _trunk(x_sample, state_rglru_h, state_rglru_conv,
                                    state_conformer_conv, w)
    return (y_p, y_s, p_h, p_rgc, p_cfc, s_h, s_rgc, s_cfc)
```
